```python
import jax, jax.numpy as jnp
from jax import lax
import numpy as np

D_MODEL = 2048
BATCH = 8
SEQ = 2048
DEPTH = 2

D_MIX = D_MODEL
D_RNN = D_MIX // 2
D_RET = D_MIX - D_RNN
LRU_BLOCKS = 8
LRU_BLOCK = D_RNN // LRU_BLOCKS
CONV_WIDTH = 4
LRU_C = 8.0
RET_HEADS = 8
RET_DV = D_RET // RET_HEADS
RET_DK = RET_DV // 2
RET_CHUNK = 128
ROPE_BASE = 10000.0
D_FF = 256 * ((8 * D_MODEL // 3 + 255) // 256)
N_EXPERTS = 8
TOP_K = 2
N_DENSE = (DEPTH + 1) // 2
N_MOE = DEPTH // 2
EPS = 1e-6
IN_SIZES = (D_RNN, D_RNN, RET_HEADS * RET_DK, RET_HEADS * RET_DK, D_RET, D_RET)
D_IN = sum(IN_SIZES)
IN_OFFSETS = tuple(int(o) for o in np.cumsum(IN_SIZES)[:-1])

kernel_name = 'hymba_style_rglru_retention_moe_trunk'


def rmsnorm(x, g):
    xf = x.astype(jnp.float32)
    y = xf * lax.rsqrt(jnp.mean(xf * xf, axis=-1, keepdims=True) + EPS)
    return (y * g.astype(jnp.float32)).astype(x.dtype)


def causal_depthwise_conv(x, w, b):
    S = x.shape[1]
    xp = jnp.pad(x, ((0, 0), (CONV_WIDTH - 1, 0), (0, 0)))
    y = b
    for tap in range(CONV_WIDTH):
        y = y + xp[:, tap:tap + S] * w[tap]
    return y


def block_diag_linear(x, w, b):
    B, S, C = x.shape
    y = jnp.einsum('bsni,nij->bsnj', x.reshape(B, S, LRU_BLOCKS, LRU_BLOCK), w)
    return y.reshape(B, S, C) + b


def _linear_recurrence_op(left, right):
    a_l, b_l = left
    a_r, b_r = right
    return a_l * a_r, a_r * b_l + b_r


def rg_lru(x, wa, ba, wx, bx, lam):
    f32 = jnp.float32
    xf = x.astype(f32)
    r = jax.nn.sigmoid(block_diag_linear(xf, wa.astype(f32), ba.astype(f32)))
    i = jax.nn.sigmoid(block_diag_linear(xf, wx.astype(f32), bx.astype(f32)))
    log_a = -LRU_C * r * jax.nn.softplus(-lam.astype(f32))
    a = jnp.exp(log_a)
    mult = jnp.sqrt(-jnp.expm1(2.0 * log_a))
    is_start = (jnp.arange(x.shape[1]) == 0)[None, :, None]
    mult = jnp.where(is_start, 1.0, mult)
    _, h = lax.associative_scan(_linear_recurrence_op, (a, mult * i * xf), axis=1)
    return h


def apply_rope(x, cos, sin):
    x1, x2 = jnp.split(x, 2, axis=-1)
    return jnp.concatenate([x1 * cos - x2 * sin, x1 * sin + x2 * cos], axis=-1)


def chunkwise_retention(q, k, v):
    B, S, H, DK = q.shape
    DV = v.shape[-1]
    C = RET_CHUNK
    N = S // C
    f32 = jnp.float32
    log_g = jnp.log(1.0 - 2.0 ** (-5.0 - jnp.arange(H, dtype=f32)))
    idx = jnp.arange(C, dtype=f32)
    rel = idx[:, None] - idx[None, :]
    decay = jnp.where(rel >= 0, jnp.exp(jnp.maximum(rel, 0.0)[None] * log_g[:, None, None]), 0.0)
    qc = q.reshape(B, N, C, H, DK)
    kc = k.reshape(B, N, C, H, DK)
    vc = v.reshape(B, N, C, H, DV)
    scores = jnp.einsum('bnihd,bnjhd->bnhij', qc, kc) * decay
    inner = jnp.einsum('bnhij,bnjhe->bnihe', scores, vc)
    zeta = jnp.exp((C - 1.0 - idx)[None, :] * log_g[:, None])
    kv = jnp.einsum('bnjhd,hj,bnjhe->nbhde', kc, zeta, vc)
    chunk_decay = jnp.exp(C * log_g)[None, :, None, None]

    def step(state, kv_n):
        return chunk_decay * state + kv_n, state

    _, prev = lax.scan(step, jnp.zeros((B, H, DK, DV), f32), kv)
    xi = jnp.exp((idx + 1.0)[None, :] * log_g[:, None])
    cross = jnp.einsum('bnihd,hi,nbhde->bnihe', qc, xi, prev)
    return (inner + cross).reshape(B, S, H, DV)


def head_groupnorm(o, g):
    B, S, H, DV = o.shape
    mu = jnp.mean(o, axis=-1, keepdims=True)
    var = jnp.mean(jnp.square(o - mu), axis=-1, keepdims=True)
    y = ((o - mu) * lax.rsqrt(var + EPS)).reshape(B, S, H * DV)
    return y * g.astype(jnp.float32)


def hybrid_mixer(h, w_in, conv_w, conv_b, lru_wa, lru_ba, lru_wx, lru_bx, lru_lambda,
                 lru_norm_g, ret_norm_g, w_out, cos, sin):
    B, S, _ = h.shape
    f32 = jnp.float32
    proj = h @ w_in
    x_rnn, gate_rnn, q, k, v, g = jnp.split(proj, IN_OFFSETS, axis=-1)
    xc = causal_depthwise_conv(x_rnn, conv_w, conv_b)
    h_lru = rg_lru(xc, lru_wa, lru_ba, lru_wx, lru_bx, lru_lambda)
    y_lru = rmsnorm(h_lru, lru_norm_g) * jax.nn.gelu(gate_rnn.astype(f32))
    qf = apply_rope(q.astype(f32).reshape(B, S, RET_HEADS, RET_DK), cos, sin)
    kf = apply_rope(k.astype(f32).reshape(B, S, RET_HEADS, RET_DK), cos, sin) * (RET_DK ** -0.5)
    vf = v.astype(f32).reshape(B, S, RET_HEADS, RET_DV)
    o = chunkwise_retention(qf, kf, vf)
    y_ret = head_groupnorm(o, ret_norm_g) * jax.nn.silu(g.astype(f32))
    y = jnp.concatenate([y_lru, y_ret], axis=-1).astype(h.dtype)
    return y @ w_out


def swiglu(h, w_gate, w_up, w_down):
    return (jax.nn.silu(h @ w_gate) * (h @ w_up)) @ w_down


def moe_swiglu(h, router, w_gate, w_up, w_down):
    shape = h.shape
    hf = h.reshape(-1, shape[-1])
    logits = (hf @ router).astype(jnp.float32)
    top_vals, top_idx = lax.top_k(logits, TOP_K)
    gates = jax.nn.softmax(top_vals, axis=-1)
    combine = jnp.einsum('tk,tke->te', gates, jax.nn.one_hot(top_idx, N_EXPERTS, dtype=jnp.float32))
    y = jnp.zeros(hf.shape, jnp.float32)
    for e in range(N_EXPERTS):
        y = y + combine[:, e:e + 1] * swiglu(hf, w_gate[e], w_up[e], w_down[e]).astype(jnp.float32)
    return y.reshape(shape).astype(h.dtype)


def setup_inputs(seed: int = 0) -> dict:
    key = jax.random.key(seed)
    ks = jax.random.split(key, 22)
    f32 = jnp.float32

    def nrm(k, shape, scale):
        return jax.random.normal(k, shape, f32) * scale

    a_c = jax.random.uniform(ks[9], (DEPTH, D_RNN), f32, 0.9, 0.999)
    s = a_c ** (1.0 / LRU_C)
    return {
        'x': nrm(ks[0], (BATCH, SEQ, D_MODEL), 1.0),
        'norm1_g': 1.0 + nrm(ks[1], (DEPTH, D_MODEL), 0.02),
        'w_in': nrm(ks[2], (DEPTH, D_MODEL, D_IN), D_MODEL ** -0.5),
        'conv_w': nrm(ks[3], (DEPTH, CONV_WIDTH, D_RNN), CONV_WIDTH ** -0.5),
        'conv_b': nrm(ks[4], (DEPTH, D_RNN), 0.01),
        'lru_wa': nrm(ks[5], (DEPTH, LRU_BLOCKS, LRU_BLOCK, LRU_BLOCK), LRU_BLOCK ** -0.5),
        'lru_ba': nrm(ks[6], (DEPTH, D_RNN), 0.01),
        'lru_wx': nrm(ks[7], (DEPTH, LRU_BLOCKS, LRU_BLOCK, LRU_BLOCK), LRU_BLOCK ** -0.5),
        'lru_bx': nrm(ks[8], (DEPTH, D_RNN), 0.01),
        'lru_lambda': jnp.log(s) - jnp.log1p(-s),
        'lru_norm_g': 1.0 + nrm(ks[10], (DEPTH, D_RNN), 0.02),
        'ret_norm_g': 1.0 + nrm(ks[11], (DEPTH, D_RET), 0.02),
        'w_out': nrm(ks[12], (DEPTH, D_MIX, D_MODEL), D_MIX ** -0.5),
        'norm2_g': 1.0 + nrm(ks[13], (DEPTH, D_MODEL), 0.02),
        'ffn_w_gate': nrm(ks[14], (N_DENSE, D_MODEL, D_FF), D_MODEL ** -0.5),
        'ffn_w_up': nrm(ks[15], (N_DENSE, D_MODEL, D_FF), D_MODEL ** -0.5),
        'ffn_w_down': nrm(ks[16], (N_DENSE, D_FF, D_MODEL), D_FF ** -0.5),
        'moe_router': nrm(ks[17], (N_MOE, D_MODEL, N_EXPERTS), D_MODEL ** -0.5),
        'moe_w_gate': nrm(ks[18], (N_MOE, N_EXPERTS, D_MODEL, D_FF), D_MODEL ** -0.5),
        'moe_w_up': nrm(ks[19], (N_MOE, N_EXPERTS, D_MODEL, D_FF), D_MODEL ** -0.5),
        'moe_w_down': nrm(ks[20], (N_MOE, N_EXPERTS, D_FF, D_MODEL), D_FF ** -0.5),
        'final_norm_g': 1.0 + nrm(ks[21], (D_MODEL,), 0.02),
    }


def reference(x, norm1_g, w_in, conv_w, conv_b, lru_wa, lru_ba, lru_wx, lru_bx, lru_lambda,
              lru_norm_g, ret_norm_g, w_out, norm2_g, ffn_w_gate, ffn_w_up, ffn_w_down,
              moe_router, moe_w_gate, moe_w_up, moe_w_down, final_norm_g):
    f32 = jnp.float32
    S = x.shape[1]
    inv_freq = ROPE_BASE ** (-jnp.arange(0, RET_DK, 2, dtype=f32) / RET_DK)
    ang = jnp.arange(S, dtype=f32)[:, None] * inv_freq[None, :]
    cos = jnp.cos(ang)[:, None, :]
    sin = jnp.sin(ang)[:, None, :]
    for layer in range(DEPTH):
        h = rmsnorm(x, norm1_g[layer])
        x = x + hybrid_mixer(h, w_in[layer], conv_w[layer], conv_b[layer], lru_wa[layer],
                             lru_ba[layer], lru_wx[layer], lru_bx[layer], lru_lambda[layer],
                             lru_norm_g[layer], ret_norm_g[layer], w_out[layer], cos, sin)
        h = rmsnorm(x, norm2_g[layer])
        j = layer // 2
        if layer % 2 == 0:
            x = x + swiglu(h, ffn_w_gate[j], ffn_w_up[j], ffn_w_down[j])
        else:
            x = x + moe_swiglu(h, moe_router[j], moe_w_gate[j], moe_w_up[j], moe_w_down[j])
    return rmsnorm(x, final_norm_g)
```

```python
import functools

import jax
import jax.numpy as jnp
from jax import lax
from jax.experimental import pallas as pl
from jax.experimental.pallas import tpu as pltpu

F32 = jnp.float32
BF16 = jnp.bfloat16

D_MODEL = 2048
DEPTH = 2
D_RNN = D_MODEL // 2
D_RET = D_MODEL - D_RNN
LRU_BLOCKS = 8
LRU_BLOCK = D_RNN // LRU_BLOCKS
CONV_WIDTH = 4
LRU_C = 8.0
RET_HEADS = 8
RET_DV = D_RET // RET_HEADS
RET_DK = RET_DV // 2
RET_CHUNK = 128
ROPE_BASE = 10000.0
D_FF = 256 * ((8 * D_MODEL // 3 + 255) // 256)
N_EXPERTS = 8
TOP_K = 2
EPS = 1e-6
D_QK = RET_HEADS * RET_DK
D_IN = 2 * D_RNN + 2 * D_QK + 2 * D_RET

LANES = 128
SUBLANES = 8
VMEM_LIMIT_BYTES = 56 * 1024 * 1024

PROJ_TM = 1024
PROJ_TN = 1024
SEQ_TS = 512
OUT_TM = 512
FFN_TM = 512
FFN_TF = 512
COMB_TT = 256


def _cparams(sem):
    return pltpu.CompilerParams(dimension_semantics=sem, vmem_limit_bytes=VMEM_LIMIT_BYTES)


def _rms(x, g):
    return x * lax.rsqrt(jnp.mean(x * x, axis=-1, keepdims=True) + EPS) * g


def _norm_matmul_kernel(x_ref, g_ref, w_ref, o_ref, h_ref):
    @pl.when(pl.program_id(1) == 0)
    def _():
        h_ref[...] = _rms(x_ref[...], g_ref[...]).astype(BF16)

    o_ref[...] = jnp.dot(h_ref[...], w_ref[...], preferred_element_type=F32)


def _norm_matmul(x, g, w):
    t, d = x.shape
    n = w.shape[1]
    tm = min(PROJ_TM, t)
    tn = PROJ_TN
    return pl.pallas_call(
        _norm_matmul_kernel,
        grid=(t // tm, n // tn),
        in_specs=[
            pl.BlockSpec((tm, d), lambda i, j: (i, 0)),
            pl.BlockSpec((1, d), lambda i, j: (0, 0)),
            pl.BlockSpec((d, tn), lambda i, j: (0, j)),
        ],
        out_specs=pl.BlockSpec((tm, tn), lambda i, j: (i, j)),
        out_shape=jax.ShapeDtypeStruct((t, n), F32),
        scratch_shapes=[pltpu.VMEM((tm, d), BF16)],
        compiler_params=_cparams(("parallel", "arbitrary")),
        name="norm_in_proj",
    )(x, g, w)


def _lru_kernel(x_ref, gate_ref, cw_ref, cb_ref, wa_ref, ba_ref, wx_ref, bx_ref, lam_ref,
                ng_ref, o_ref, xbuf, a_s, b_s, hcar):
    c = pl.program_id(1)
    ts = x_ref.shape[1]
    hist = SUBLANES

    @pl.when(c == 0)
    def _():
        xbuf[0:hist, :] = jnp.zeros((hist, D_RNN), F32)
        hcar[...] = jnp.zeros((1, D_RNN), F32)

    x = x_ref[0]
    xbuf[hist:hist + ts, :] = x
    xc = cb_ref[...]
    for tap in range(CONV_WIDTH):
        off = hist - (CONV_WIDTH - 1) + tap
        xc = xc + xbuf[off:off + ts, :] * cw_ref[tap:tap + 1, :]
    xbuf[0:hist, :] = x[ts - hist:ts, :]

    xcb = xc.astype(BF16)
    ra = []
    rx = []
    for n in range(LRU_BLOCKS):
        xs = xcb[:, n * LRU_BLOCK:(n + 1) * LRU_BLOCK]
        ra.append(jnp.dot(xs, wa_ref[n], preferred_element_type=F32))
        rx.append(jnp.dot(xs, wx_ref[n], preferred_element_type=F32))
    r = jax.nn.sigmoid(jnp.concatenate(ra, axis=1) + ba_ref[...])
    ig = jax.nn.sigmoid(jnp.concatenate(rx, axis=1) + bx_ref[...])

    z = -lam_ref[...]
    softplus = jnp.maximum(z, 0.0) + jnp.log1p(jnp.exp(-jnp.abs(z)))
    log_a = (-LRU_C * r) * softplus
    a = jnp.exp(log_a)
    mult = jnp.sqrt(1.0 - a * a)
    a_s[...] = a
    b_s[...] = mult * ig * xc

    @pl.when(c == 0)
    def _():
        b_s[0:1, :] = ig[0:1, :] * xc[0:1, :]

    rows8 = lax.broadcasted_iota(jnp.int32, (SUBLANES, D_RNN), 0)

    def body(i, h):
        r0 = pl.multiple_of(i * SUBLANES, SUBLANES)
        av = a_s[pl.ds(r0, SUBLANES), :]
        bv = b_s[pl.ds(r0, SUBLANES), :]
        for sh in (1, 2, 4):
            a_sh = pltpu.roll(av, sh, axis=0)
            b_sh = pltpu.roll(bv, sh, axis=0)
            m = rows8 >= sh
            bv = jnp.where(m, av * b_sh + bv, bv)
            av = jnp.where(m, av * a_sh, av)
        hv = av * h + bv
        b_s[pl.ds(r0, SUBLANES), :] = hv
        return hv[SUBLANES - 1:SUBLANES, :]

    hcar[...] = lax.fori_loop(0, ts // SUBLANES, body, hcar[...])

    y = _rms(b_s[...], ng_ref[...])
    o_ref[0] = (y * jax.nn.gelu(gate_ref[0])).astype(BF16)


def _lru_branch(proj3, cw, cb, wa, ba, wx, bx, lam, ng):
    b, s, _ = proj3.shape
    ts = min(SEQ_TS, s)
    vec = pl.BlockSpec((1, D_RNN), lambda i, j: (0, 0))
    blk = pl.BlockSpec((LRU_BLOCKS, LRU_BLOCK, LRU_BLOCK), lambda i, j: (0, 0, 0))
    return pl.pallas_call(
        _lru_kernel,
        grid=(b, s // ts),
        in_specs=[
            pl.BlockSpec((1, ts, D_RNN), lambda i, j: (i, j, 0)),
            pl.BlockSpec((1, ts, D_RNN), lambda i, j: (i, j, 1)),
            pl.BlockSpec((CONV_WIDTH, D_RNN), lambda i, j: (0, 0)),
            vec, blk, vec, blk, vec, vec, vec,
        ],
        out_specs=pl.BlockSpec((1, ts, D_RNN), lambda i, j: (i, j, 0)),
        out_shape=jax.ShapeDtypeStruct((b, s, D_RNN), BF16),
        scratch_shapes=[
            pltpu.VMEM((ts + SUBLANES, D_RNN), F32),
            pltpu.VMEM((ts, D_RNN), F32),
            pltpu.VMEM((ts, D_RNN), F32),
            pltpu.VMEM((1, D_RNN), F32),
        ],
        compiler_params=_cparams(("parallel", "arbitrary")),
        name="rglru_branch",
    )(proj3, proj3, cw, cb, wa, ba, wx, bx, lam, ng)


def _rope(x, cos, sin_signed):
    half = RET_DK // 2
    width = x.shape[1]
    lane = lax.broadcasted_iota(jnp.int32, x.shape, 1)
    first = (lane & (RET_DK - 1)) < half
    rot = jnp.where(first, pltpu.roll(x, width - half, axis=1), pltpu.roll(x, half, axis=1))
    return x * cos + rot * sin_signed


def _ret_kernel(q_ref, k_ref, v_ref, g_ref, cos_ref, sin_ref, xi_ref, zeta_ref, decay_ref,
                cdec_ref, gn_ref, o_ref, state):
    c = pl.program_id(1)
    ts = q_ref.shape[1]

    @pl.when(c == 0)
    def _():
        state[...] = jnp.zeros(state.shape, F32)

    cos = cos_ref[...]
    sin = sin_ref[...]
    q = _rope(q_ref[0], cos, sin)
    k = _rope(k_ref[0], cos, sin) * (RET_DK ** -0.5)
    for n in range(ts // RET_CHUNK):
        lo = n * RET_CHUNK
        qn = q[lo:lo + RET_CHUNK, :]
        kn = k[lo:lo + RET_CHUNK, :]
        q_in = qn.astype(BF16)
        k_in = kn.astype(BF16)
        q_x = (qn * xi_ref[...]).astype(BF16)
        k_z = (kn * zeta_ref[...]).astype(BF16)
        for h in range(RET_HEADS):
            ks = slice(h * RET_DK, (h + 1) * RET_DK)
            vs = slice(h * RET_DV, (h + 1) * RET_DV)
            vh = v_ref[0, lo:lo + RET_CHUNK, vs].astype(BF16)
            scores = lax.dot_general(q_in[:, ks], k_in[:, ks], (((1,), (1,)), ((), ())),
                                     preferred_element_type=F32) * decay_ref[h]
            inner = jnp.dot(scores.astype(BF16), vh, preferred_element_type=F32)
            st = state[h]
            cross = jnp.dot(q_x[:, ks], st.astype(BF16), preferred_element_type=F32)
            kv = lax.dot_general(k_z[:, ks], vh, (((0,), (0,)), ((), ())),
                                 preferred_element_type=F32)
            state[h] = cdec_ref[h] * st + kv
            o = inner + cross
            mu = jnp.mean(o, axis=-1, keepdims=True)
            d = o - mu
            var = jnp.mean(d * d, axis=-1, keepdims=True)
            y = d * lax.rsqrt(var + EPS) * gn_ref[:, vs]
            gate = g_ref[0, lo:lo + RET_CHUNK, vs]
            o_ref[0, lo:lo + RET_CHUNK, vs] = (y * jax.nn.silu(gate)).astype(BF16)


def _ret_tables(s):
    inv_freq = ROPE_BASE ** (-jnp.arange(0, RET_DK, 2, dtype=F32) / RET_DK)
    ang = jnp.arange(s, dtype=F32)[:, None] * inv_freq[None, :]
    cos = jnp.cos(ang)
    sin = jnp.sin(ang)
    cos_t = jnp.tile(jnp.concatenate([cos, cos], axis=1), (1, RET_HEADS))
    sin_t = jnp.tile(jnp.concatenate([-sin, sin], axis=1), (1, RET_HEADS))
    log_g = jnp.log(1.0 - 2.0 ** (-5.0 - jnp.arange(RET_HEADS, dtype=F32)))
    idx = jnp.arange(RET_CHUNK, dtype=F32)
    rel = idx[:, None] - idx[None, :]
    decay = jnp.where(rel >= 0, jnp.exp(jnp.maximum(rel, 0.0)[None] * log_g[:, None, None]), 0.0)
    zeta = jnp.exp((RET_CHUNK - 1.0 - idx)[None, :] * log_g[:, None])
    xi = jnp.exp((idx + 1.0)[None, :] * log_g[:, None])
    zeta_t = jnp.repeat(zeta.T, RET_DK, axis=1)
    xi_t = jnp.repeat(xi.T, RET_DK, axis=1)
    cdec = jnp.broadcast_to(jnp.exp(RET_CHUNK * log_g)[:, None, None], (RET_HEADS, RET_DK, RET_DV))
    return cos_t, sin_t, xi_t, zeta_t, decay, cdec


def _ret_branch(proj3, tables, gn):
    b, s, _ = proj3.shape
    ts = min(SEQ_TS, s)
    cos_t, sin_t, xi_t, zeta_t, decay, cdec = tables
    q_blk = 2 * D_RNN // D_QK
    return pl.pallas_call(
        _ret_kernel,
        grid=(b, s // ts),
        in_specs=[
            pl.BlockSpec((1, ts, D_QK), lambda i, j: (i, j, q_blk)),
            pl.BlockSpec((1, ts, D_QK), lambda i, j: (i, j, q_blk + 1)),
            pl.BlockSpec((1, ts, D_RET), lambda i, j: (i, j, 3)),
            pl.BlockSpec((1, ts, D_RET), lambda i, j: (i, j, 4)),
            pl.BlockSpec((ts, D_QK), lambda i, j: (j, 0)),
            pl.BlockSpec((ts, D_QK), lambda i, j: (j, 0)),
            pl.BlockSpec((RET_CHUNK, D_QK), lambda i, j: (0, 0)),
            pl.BlockSpec((RET_CHUNK, D_QK), lambda i, j: (0, 0)),
            pl.BlockSpec((RET_HEADS, RET_CHUNK, RET_CHUNK), lambda i, j: (0, 0, 0)),
            pl.BlockSpec((RET_HEADS, RET_DK, RET_DV), lambda i, j: (0, 0, 0)),
            pl.BlockSpec((1, D_RET), lambda i, j: (0, 0)),
        ],
        out_specs=pl.BlockSpec((1, ts, D_RET), lambda i, j: (i, j, 0)),
        out_shape=jax.ShapeDtypeStruct((b, s, D_RET), BF16),
        scratch_shapes=[pltpu.VMEM((RET_HEADS, RET_DK, RET_DV), F32)],
        compiler_params=_cparams(("parallel", "arbitrary")),
        name="retention_branch",
    )(proj3, proj3, proj3, proj3, cos_t, sin_t, xi_t, zeta_t, decay, cdec, gn)


def _out_proj_kernel(yl_ref, yr_ref, x_ref, w_ref, g_ref, x1_ref, h_ref):
    acc = jnp.dot(yl_ref[...], w_ref[0:D_RNN, :], preferred_element_type=F32)
    acc = acc + jnp.dot(yr_ref[...], w_ref[D_RNN:D_MODEL, :], preferred_element_type=F32)
    x1 = x_ref[...] + acc
    x1_ref[...] = x1
    h_ref[...] = _rms(x1, g_ref[...]).astype(BF16)


def _out_proj_router_kernel(yl_ref, yr_ref, x_ref, w_ref, g_ref, rw_ref, x1_ref, idx_ref, gate_ref):
    acc = jnp.dot(yl_ref[...], w_ref[0:D_RNN, :], preferred_element_type=F32)
    acc = acc + jnp.dot(yr_ref[...], w_ref[D_RNN:D_MODEL, :], preferred_element_type=F32)
    x1 = x_ref[...] + acc
    x1_ref[...] = x1
    h = _rms(x1, g_ref[...]).astype(BF16)
    logits = jnp.dot(h, rw_ref[...], preferred_element_type=F32)
    lane_i = lax.broadcasted_iota(jnp.int32, logits.shape, 1)
    lane = lane_i.astype(F32)
    neg = jnp.float32(-jnp.inf)
    logits = jnp.where(lane_i < N_EXPERTS, logits, neg)
    m1 = jnp.max(logits, axis=-1, keepdims=True)
    i1 = jnp.min(jnp.where(logits == m1, lane, float(LANES)), axis=-1, keepdims=True)
    rest = jnp.where(lane == i1, neg, logits)
    m2 = jnp.max(rest, axis=-1, keepdims=True)
    i2 = jnp.min(jnp.where(rest == m2, lane, float(LANES)), axis=-1, keepdims=True)
    e2 = jnp.exp(m2 - m1)
    den = 1.0 + e2
    idx_ref[...] = jnp.where(lane_i == 0, i1, i2).astype(jnp.int32)
    gate_ref[...] = jnp.where(lane_i == 0, 1.0 / den, e2 / den)


def _out_proj(yl, yr, x, w, g, router=None):
    t, d = x.shape
    tm = min(OUT_TM, t)
    row = lambda i: (i, 0)
    fixed = lambda i: (0, 0)
    in_specs = [
        pl.BlockSpec((tm, D_RNN), row),
        pl.BlockSpec((tm, D_RET), row),
        pl.BlockSpec((tm, d), row),
        pl.BlockSpec((d, d), fixed),
        pl.BlockSpec((1, d), fixed),
    ]
    if router is None:
        return pl.pallas_call(
            _out_proj_kernel,
            grid=(t // tm,),
            in_specs=in_specs,
            out_specs=[pl.BlockSpec((tm, d), row), pl.BlockSpec((tm, d), row)],
            out_shape=[jax.ShapeDtypeStruct((t, d), F32), jax.ShapeDtypeStruct((t, d), BF16)],
            compiler_params=_cparams(("parallel",)),
            name="out_proj_norm",
        )(yl, yr, x, w, g)
    return pl.pallas_call(
        _out_proj_router_kernel,
        grid=(t // tm,),
        in_specs=in_specs + [pl.BlockSpec((d, LANES), fixed)],
        out_specs=[pl.BlockSpec((tm, d), row), pl.BlockSpec((tm, LANES), row),
                   pl.BlockSpec((tm, LANES), row)],
        out_shape=[jax.ShapeDtypeStruct((t, d), F32), jax.ShapeDtypeStruct((t, LANES), jnp.int32),
                   jax.ShapeDtypeStruct((t, LANES), F32)],
        compiler_params=_cparams(("parallel",)),
        name="out_proj_router",
    )(yl, yr, x, w, g, router)


def _swiglu_step(h, wg, wu, wd):
    gt = jnp.dot(h, wg, preferred_element_type=F32)
    up = jnp.dot(h, wu, preferred_element_type=F32)
    act = (jax.nn.silu(gt) * up).astype(BF16)
    return jnp.dot(act, wd, preferred_element_type=F32)


def _ffn_kernel(h_ref, x_ref, wg_ref, wu_ref, wd_ref, o_ref):
    f = pl.program_id(1)
    part = _swiglu_step(h_ref[...], wg_ref[...], wu_ref[...], wd_ref[...])

    @pl.when(f == 0)
    def _():
        o_ref[...] = x_ref[...] + part

    @pl.when(f > 0)
    def _():
        o_ref[...] += part


def _dense_ffn(h, x, wg, wu, wd):
    t, d = x.shape
    tm = min(FFN_TM, t)
    tf = FFN_TF
    return pl.pallas_call(
        _ffn_kernel,
        grid=(t // tm, D_FF // tf),
        in_specs=[
            pl.BlockSpec((tm, d), lambda i, f: (i, 0)),
            pl.BlockSpec((tm, d), lambda i, f: (i, 0)),
            pl.BlockSpec((d, tf), lambda i, f: (0, f)),
            pl.BlockSpec((d, tf), lambda i, f: (0, f)),
            pl.BlockSpec((tf, d), lambda i, f: (f, 0)),
        ],
        out_specs=pl.BlockSpec((tm, d), lambda i, f: (i, 0)),
        out_shape=jax.ShapeDtypeStruct((t, d), F32),
        compiler_params=_cparams(("parallel", "arbitrary")),
        name="dense_swiglu",
    )(h, x, wg, wu, wd)


def _moe_kernel(te_ref, tv_ref, src_ref, x_hbm, g_ref, wg_ref, wu_ref, wd_ref, o_ref,
                xg, hs, sem):
    i = pl.program_id(0)
    f = pl.program_id(1)
    tm = xg.shape[0]
    valid = tv_ref[i] == 1

    def row_copy(r):
        tok = src_ref[0, 0, r]
        return pltpu.make_async_copy(x_hbm.at[pl.ds(tok, 1)], xg.at[pl.ds(r, 1)], sem.at[0])

    @pl.when(valid & (f == 0))
    def _():
        def start(r, carry):
            row_copy(r).start()
            return carry

        lax.fori_loop(0, tm, start, 0)

        def wait(r, carry):
            row_copy(r).wait()
            return carry

        lax.fori_loop(0, tm, wait, 0)
        hs[...] = _rms(xg[...], g_ref[...]).astype(BF16)

    @pl.when(valid)
    def _():
        part = _swiglu_step(hs[...], wg_ref[0], wu_ref[0], wd_ref[0])

        @pl.when(f == 0)
        def _():
            o_ref[...] = part

        @pl.when(f > 0)
        def _():
            o_ref[...] += part

    @pl.when(jnp.logical_not(valid) & (f == 0))
    def _():
        o_ref[...] = jnp.zeros(o_ref.shape, F32)


def _moe_ffn(x, g, tile_expert, tile_valid, src3, wg, wu, wd):
    t, d = x.shape
    n_tiles, _, tm = src3.shape
    tf = FFN_TF
    nf = D_FF // tf

    def w_col(i, f, te, tv):
        return (te[i], 0, jnp.where(tv[i] == 1, f, nf - 1))

    def w_row(i, f, te, tv):
        return (te[i], jnp.where(tv[i] == 1, f, nf - 1), 0)

    grid_spec = pltpu.PrefetchScalarGridSpec(
        num_scalar_prefetch=2,
        grid=(n_tiles, nf),
        in_specs=[
            pl.BlockSpec((1, 1, tm), lambda i, f, te, tv: (i, 0, 0), memory_space=pltpu.SMEM),
            pl.BlockSpec(memory_space=pl.ANY),
            pl.BlockSpec((1, d), lambda i, f, te, tv: (0, 0)),
            pl.BlockSpec((1, d, tf), w_col),
            pl.BlockSpec((1, d, tf), w_col),
            pl.BlockSpec((1, tf, d), w_row),
        ],
        out_specs=pl.BlockSpec((tm, d), lambda i, f, te, tv: (i, 0)),
        scratch_shapes=[
            pltpu.VMEM((tm, d), F32),
            pltpu.VMEM((tm, d), BF16),
            pltpu.SemaphoreType.DMA((1,)),
        ],
    )
    return pl.pallas_call(
        _moe_kernel,
        grid_spec=grid_spec,
        out_shape=jax.ShapeDtypeStruct((n_tiles * tm, d), F32),
        compiler_params=_cparams(("arbitrary", "arbitrary")),
        name="expert_swiglu",
    )(tile_expert, tile_valid, src3, x, g, wg, wu, wd)


def _combine_kernel(pos_ref, x_ref, gate_ref, g_ref, ys_hbm, o_ref, buf, sem):
    tt = x_ref.shape[0]

    def row_copy(r, k):
        p = pos_ref[0, 0, TOP_K * r + k]
        return pltpu.make_async_copy(ys_hbm.at[pl.ds(p, 1)], buf.at[k, pl.ds(r, 1)], sem.at[0])

    def start(r, carry):
        for k in range(TOP_K):
            row_copy(r, k).start()
        return carry

    lax.fori_loop(0, tt, start, 0)

    def wait(r, carry):
        for k in range(TOP_K):
            row_copy(r, k).wait()
        return carry

    lax.fori_loop(0, tt, wait, 0)
    gates = gate_ref[...]
    y = x_ref[...]
    moe = gates[:, 0:1] * buf[0]
    for k in range(1, TOP_K):
        moe = moe + gates[:, k:k + 1] * buf[k]
    o_ref[...] = _rms(y + moe, g_ref[...])


def _combine(x, gates, pos3, ys, g):
    t, d = x.shape
    n_steps, _, per = pos3.shape
    tt = per // TOP_K
    return pl.pallas_call(
        _combine_kernel,
        grid=(n_steps,),
        in_specs=[
            pl.BlockSpec((1, 1, per), lambda i: (i, 0, 0), memory_space=pltpu.SMEM),
            pl.BlockSpec((tt, d), lambda i: (i, 0)),
            pl.BlockSpec((tt, LANES), lambda i: (i, 0)),
            pl.BlockSpec((1, d), lambda i: (0, 0)),
            pl.BlockSpec(memory_space=pl.ANY),
        ],
        out_specs=pl.BlockSpec((tt, d), lambda i: (i, 0)),
        out_shape=jax.ShapeDtypeStruct((t, d), F32),
        scratch_shapes=[pltpu.VMEM((TOP_K, tt, d), F32), pltpu.SemaphoreType.DMA((1,))],
        compiler_params=_cparams(("arbitrary",)),
        name="combine_final_norm",
    )(pos3, x, gates, g, ys)


def _routing_plan(top_idx, tm):
    t = top_idx.shape[0]
    n_slots = t * TOP_K
    e_flat = top_idx.reshape(n_slots)
    onehot = (e_flat[:, None] == jnp.arange(N_EXPERTS, dtype=jnp.int32)[None, :]).astype(jnp.int32)
    csum = jnp.cumsum(onehot, axis=0)
    counts = csum[-1]
    rank = jnp.sum((csum - onehot) * onehot, axis=1)
    padded = ((counts + tm - 1) // tm) * tm
    ends = jnp.cumsum(padded)
    starts = ends - padded
    pos = jnp.sum(starts[None, :] * onehot, axis=1) + rank
    n_tiles = n_slots // tm + N_EXPERTS
    src = jnp.zeros((n_tiles * tm,), jnp.int32).at[pos].set(
        jnp.arange(n_slots, dtype=jnp.int32) // TOP_K)
    tile_start = jnp.arange(n_tiles, dtype=jnp.int32) * tm
    tile_expert = jnp.minimum(
        jnp.sum((tile_start[:, None] >= ends[None, :]).astype(jnp.int32), axis=1), N_EXPERTS - 1)
    tile_valid = (tile_start < ends[-1]).astype(jnp.int32)
    return pos, src.reshape(n_tiles, 1, tm), tile_expert, tile_valid


def kernel(x, norm1_g, w_in, conv_w, conv_b, lru_wa, lru_ba, lru_wx, lru_bx, lru_lambda,
           lru_norm_g, ret_norm_g, w_out, norm2_g, ffn_w_gate, ffn_w_up, ffn_w_down,
           moe_router, moe_w_gate, moe_w_up, moe_w_down, final_norm_g):
    b, s, d = x.shape
    t = b * s
    tables = _ret_tables(s)
    xf = x.reshape(t, d)
    row = lambda v: v.reshape(1, -1)
    out = None
    for layer in range(DEPTH):
        proj = _norm_matmul(xf, row(norm1_g[layer]), w_in[layer].astype(BF16))
        proj3 = proj.reshape(b, s, D_IN)
        y_lru = _lru_branch(proj3, conv_w[layer], row(conv_b[layer]),
                            lru_wa[layer].astype(BF16), row(lru_ba[layer]),
                            lru_wx[layer].astype(BF16), row(lru_bx[layer]),
                            row(lru_lambda[layer]), row(lru_norm_g[layer]))
        y_ret = _ret_branch(proj3, tables, row(ret_norm_g[layer]))
        yl = y_lru.reshape(t, D_RNN)
        yr = y_ret.reshape(t, D_RET)
        w_o = w_out[layer].astype(BF16)
        j = layer // 2
        if layer % 2 == 0:
            x1, h2 = _out_proj(yl, yr, xf, w_o, row(norm2_g[layer]))
            xf = _dense_ffn(h2, x1, ffn_w_gate[j].astype(BF16), ffn_w_up[j].astype(BF16),
                            ffn_w_down[j].astype(BF16))
        else:
            router = jnp.pad(moe_router[j], ((0, 0), (0, LANES - N_EXPERTS))).astype(BF16)
            x1, idx, gates = _out_proj(yl, yr, xf, w_o, row(norm2_g[layer]), router)
            tm = min(FFN_TM, t)
            pos, src3, tile_expert, tile_valid = _routing_plan(idx[:, :TOP_K], tm)
            ys = _moe_ffn(x1, row(norm2_g[layer]), tile_expert, tile_valid, src3,
                          moe_w_gate[j].astype(BF16), moe_w_up[j].astype(BF16),
                          moe_w_down[j].astype(BF16))
            tt = min(COMB_TT, t)
            pos3 = pos.reshape(t // tt, 1, tt * TOP_K)
            out = _combine(x1, gates, pos3, ys, row(final_norm_g))
    return out.reshape(b, s, d)
```

```python
import functools

import jax
import jax.numpy as jnp
from jax import lax
from jax.experimental import pallas as pl
from jax.experimental.pallas import tpu as pltpu

F32 = jnp.float32
BF16 = jnp.bfloat16

D_MODEL = 2048
DEPTH = 2
D_RNN = D_MODEL // 2
D_RET = D_MODEL - D_RNN
LRU_BLOCKS = 8
LRU_BLOCK = D_RNN // LRU_BLOCKS
CONV_WIDTH = 4
LRU_C = 8.0
RET_HEADS = 8
RET_DV = D_RET // RET_HEADS
RET_DK = RET_DV // 2
RET_CHUNK = 128
ROPE_BASE = 10000.0
D_FF = 256 * ((8 * D_MODEL // 3 + 255) // 256)
N_EXPERTS = 8
TOP_K = 2
EPS = 1e-6
D_QK = RET_HEADS * RET_DK
D_IN = 2 * D_RNN + 2 * D_QK + 2 * D_RET

LANES = 128
SUBLANES = 8
VMEM_LIMIT_BYTES = 56 * 1024 * 1024

PROJ_TM = 1024
PROJ_TN = 1024
SEQ_TS = 512
OUT_TM = 512
FFN_TM = 512
FFN_TF = 512
COMB_TT = 256


def _cparams(sem):
    return pltpu.CompilerParams(dimension_semantics=sem, vmem_limit_bytes=VMEM_LIMIT_BYTES)


def _rms(x, g):
    return x * lax.rsqrt(jnp.mean(x * x, axis=-1, keepdims=True) + EPS) * g


def _norm_matmul_kernel(x_ref, g_ref, w_ref, o_ref, h_ref):
    @pl.when(pl.program_id(1) == 0)
    def _():
        h_ref[...] = _rms(x_ref[...], g_ref[...]).astype(BF16)

    o_ref[...] = jnp.dot(h_ref[...], w_ref[...], preferred_element_type=F32)


def _norm_matmul(x, g, w):
    t, d = x.shape
    n = w.shape[1]
    tm = min(PROJ_TM, t)
    tn = PROJ_TN
    return pl.pallas_call(
        _norm_matmul_kernel,
        grid=(t // tm, n // tn),
        in_specs=[
            pl.BlockSpec((tm, d), lambda i, j: (i, 0)),
            pl.BlockSpec((1, d), lambda i, j: (0, 0)),
            pl.BlockSpec((d, tn), lambda i, j: (0, j)),
        ],
        out_specs=pl.BlockSpec((tm, tn), lambda i, j: (i, j)),
        out_shape=jax.ShapeDtypeStruct((t, n), F32),
        scratch_shapes=[pltpu.VMEM((tm, d), BF16)],
        compiler_params=_cparams(("parallel", "arbitrary")),
        name="norm_in_proj",
    )(x, g, w)


def _lru_kernel(x_ref, gate_ref, cw_ref, cb_ref, wa_ref, ba_ref, wx_ref, bx_ref, lam_ref,
                ng_ref, o_ref, xbuf, a_s, b_s, hcar):
    c = pl.program_id(1)
    ts = x_ref.shape[1]
    hist = SUBLANES

    @pl.when(c == 0)
    def _():
        xbuf[0:hist, :] = jnp.zeros((hist, D_RNN), F32)
        hcar[...] = jnp.zeros((1, D_RNN), F32)

    x = x_ref[0]
    xbuf[hist:hist + ts, :] = x
    xc = cb_ref[...]
    for tap in range(CONV_WIDTH):
        off = hist - (CONV_WIDTH - 1) + tap
        xc = xc + xbuf[off:off + ts, :] * cw_ref[tap:tap + 1, :]
    xbuf[0:hist, :] = x[ts - hist:ts, :]

    xcb = xc.astype(BF16)
    ra = []
    rx = []
    for n in range(LRU_BLOCKS):
        xs = xcb[:, n * LRU_BLOCK:(n + 1) * LRU_BLOCK]
        ra.append(jnp.dot(xs, wa_ref[n], preferred_element_type=F32))
        rx.append(jnp.dot(xs, wx_ref[n], preferred_element_type=F32))
    r = jax.nn.sigmoid(jnp.concatenate(ra, axis=1) + ba_ref[...])
    ig = jax.nn.sigmoid(jnp.concatenate(rx, axis=1) + bx_ref[...])

    z = -lam_ref[...]
    softplus = jnp.maximum(z, 0.0) + jnp.log1p(jnp.exp(-jnp.abs(z)))
    log_a = (-LRU_C * r) * softplus
    a = jnp.exp(log_a)
    mult = jnp.sqrt(1.0 - a * a)
    a_s[...] = a
    b_s[...] = mult * ig * xc

    @pl.when(c == 0)
    def _():
        b_s[0:1, :] = ig[0:1, :] * xc[0:1, :]

    rows8 = lax.broadcasted_iota(jnp.int32, (SUBLANES, D_RNN), 0)

    def body(i, h):
        r0 = pl.multiple_of(i * SUBLANES, SUBLANES)
        av = a_s[pl.ds(r0, SUBLANES), :]
        bv = b_s[pl.ds(r0, SUBLANES), :]
        for sh in (1, 2, 4):
            a_sh = pltpu.roll(av, sh, axis=0)
            b_sh = pltpu.roll(bv, sh, axis=0)
            m = rows8 >= sh
            bv = jnp.where(m, av * b_sh + bv, bv)
            av = jnp.where(m, av * a_sh, av)
        hv = av * h + bv
        b_s[pl.ds(r0, SUBLANES), :] = hv
        return hv[SUBLANES - 1:SUBLANES, :]

    hcar[...] = lax.fori_loop(0, ts // SUBLANES, body, hcar[...])

    y = _rms(b_s[...], ng_ref[...])
    o_ref[0] = (y * jax.nn.gelu(gate_ref[0])).astype(BF16)


def _lru_branch(proj3, cw, cb, wa, ba, wx, bx, lam, ng):
    b, s, _ = proj3.shape
    ts = min(SEQ_TS, s)
    vec = pl.BlockSpec((1, D_RNN), lambda i, j: (0, 0))
    blk = pl.BlockSpec((LRU_BLOCKS, LRU_BLOCK, LRU_BLOCK), lambda i, j: (0, 0, 0))
    return pl.pallas_call(
        _lru_kernel,
        grid=(b, s // ts),
        in_specs=[
            pl.BlockSpec((1, ts, D_RNN), lambda i, j: (i, j, 0)),
            pl.BlockSpec((1, ts, D_RNN), lambda i, j: (i, j, 1)),
            pl.BlockSpec((CONV_WIDTH, D_RNN), lambda i, j: (0, 0)),
            vec, blk, vec, blk, vec, vec, vec,
        ],
        out_specs=pl.BlockSpec((1, ts, D_RNN), lambda i, j: (i, j, 0)),
        out_shape=jax.ShapeDtypeStruct((b, s, D_RNN), BF16),
        scratch_shapes=[
            pltpu.VMEM((ts + SUBLANES, D_RNN), F32),
            pltpu.VMEM((ts, D_RNN), F32),
            pltpu.VMEM((ts, D_RNN), F32),
            pltpu.VMEM((1, D_RNN), F32),
        ],
        compiler_params=_cparams(("parallel", "arbitrary")),
        name="rglru_branch",
    )(proj3, proj3, cw, cb, wa, ba, wx, bx, lam, ng)


def _rope(x, cos, sin_signed):
    half = RET_DK // 2
    width = x.shape[1]
    lane = lax.broadcasted_iota(jnp.int32, x.shape, 1)
    first = (lane & (RET_DK - 1)) < half
    rot = jnp.where(first, pltpu.roll(x, width - half, axis=1), pltpu.roll(x, half, axis=1))
    return x * cos + rot * sin_signed


def _ret_kernel(q_ref, k_ref, v_ref, g_ref, cos_ref, sin_ref, xi_ref, zeta_ref, decay_ref,
                cdec_ref, gn_ref, o_ref, state):
    c = pl.program_id(1)
    ts = q_ref.shape[1]

    @pl.when(c == 0)
    def _():
        state[...] = jnp.zeros(state.shape, F32)

    cos = cos_ref[...]
    sin = sin_ref[...]
    q = _rope(q_ref[0], cos, sin)
    k = _rope(k_ref[0], cos, sin) * (RET_DK ** -0.5)
    for n in range(ts // RET_CHUNK):
        lo = n * RET_CHUNK
        qn = q[lo:lo + RET_CHUNK, :]
        kn = k[lo:lo + RET_CHUNK, :]
        q_in = qn.astype(BF16)
        k_in = kn.astype(BF16)
        q_x = (qn * xi_ref[...]).astype(BF16)
        k_z = (kn * zeta_ref[...]).astype(BF16)
        for h in range(RET_HEADS):
            ks = slice(h * RET_DK, (h + 1) * RET_DK)
            vs = slice(h * RET_DV, (h + 1) * RET_DV)
            vh = v_ref[0, lo:lo + RET_CHUNK, vs].astype(BF16)
            scores = lax.dot_general(q_in[:, ks], k_in[:, ks], (((1,), (1,)), ((), ())),
                                     preferred_element_type=F32) * decay_ref[h]
            inner = jnp.dot(scores.astype(BF16), vh, preferred_element_type=F32)
            st = state[h]
            cross = jnp.dot(q_x[:, ks], st.astype(BF16), preferred_element_type=F32)
            kv = lax.dot_general(k_z[:, ks], vh, (((0,), (0,)), ((), ())),
                                 preferred_element_type=F32)
            state[h] = cdec_ref[h] * st + kv
            o = inner + cross
            mu = jnp.mean(o, axis=-1, keepdims=True)
            d = o - mu
            var = jnp.mean(d * d, axis=-1, keepdims=True)
            y = d * lax.rsqrt(var + EPS) * gn_ref[:, vs]
            gate = g_ref[0, lo:lo + RET_CHUNK, vs]
            o_ref[0, lo:lo + RET_CHUNK, vs] = (y * jax.nn.silu(gate)).astype(BF16)


def _ret_tables(s):
    inv_freq = ROPE_BASE ** (-jnp.arange(0, RET_DK, 2, dtype=F32) / RET_DK)
    ang = jnp.arange(s, dtype=F32)[:, None] * inv_freq[None, :]
    cos = jnp.cos(ang)
    sin = jnp.sin(ang)
    cos_t = jnp.tile(jnp.concatenate([cos, cos], axis=1), (1, RET_HEADS))
    sin_t = jnp.tile(jnp.concatenate([-sin, sin], axis=1), (1, RET_HEADS))
    log_g = jnp.log(1.0 - 2.0 ** (-5.0 - jnp.arange(RET_HEADS, dtype=F32)))
    idx = jnp.arange(RET_CHUNK, dtype=F32)
    rel = idx[:, None] - idx[None, :]
    decay = jnp.where(rel >= 0, jnp.exp(jnp.maximum(rel, 0.0)[None] * log_g[:, None, None]), 0.0)
    zeta = jnp.exp((RET_CHUNK - 1.0 - idx)[None, :] * log_g[:, None])
    xi = jnp.exp((idx + 1.0)[None, :] * log_g[:, None])
    zeta_t = jnp.repeat(zeta.T, RET_DK, axis=1)
    xi_t = jnp.repeat(xi.T, RET_DK, axis=1)
    cdec = jnp.broadcast_to(jnp.exp(RET_CHUNK * log_g)[:, None, None], (RET_HEADS, RET_DK, RET_DV))
    return cos_t, sin_t, xi_t, zeta_t, decay, cdec


def _ret_branch(proj3, tables, gn):
    b, s, _ = proj3.shape
    ts = min(SEQ_TS, s)
    cos_t, sin_t, xi_t, zeta_t, decay, cdec = tables
    q_blk = 2 * D_RNN // D_QK
    return pl.pallas_call(
        _ret_kernel,
        grid=(b, s // ts),
        in_specs=[
            pl.BlockSpec((1, ts, D_QK), lambda i, j: (i, j, q_blk)),
            pl.BlockSpec((1, ts, D_QK), lambda i, j: (i, j, q_blk + 1)),
            pl.BlockSpec((1, ts, D_RET), lambda i, j: (i, j, 3)),
            pl.BlockSpec((1, ts, D_RET), lambda i, j: (i, j, 4)),
            pl.BlockSpec((ts, D_QK), lambda i, j: (j, 0)),
            pl.BlockSpec((ts, D_QK), lambda i, j: (j, 0)),
            pl.BlockSpec((RET_CHUNK, D_QK), lambda i, j: (0, 0)),
            pl.BlockSpec((RET_CHUNK, D_QK), lambda i, j: (0, 0)),
            pl.BlockSpec((RET_HEADS, RET_CHUNK, RET_CHUNK), lambda i, j: (0, 0, 0)),
            pl.BlockSpec((RET_HEADS, RET_DK, RET_DV), lambda i, j: (0, 0, 0)),
            pl.BlockSpec((1, D_RET), lambda i, j: (0, 0)),
        ],
        out_specs=pl.BlockSpec((1, ts, D_RET), lambda i, j: (i, j, 0)),
        out_shape=jax.ShapeDtypeStruct((b, s, D_RET), BF16),
        scratch_shapes=[pltpu.VMEM((RET_HEADS, RET_DK, RET_DV), F32)],
        compiler_params=_cparams(("parallel", "arbitrary")),
        name="retention_branch",
    )(proj3, proj3, proj3, proj3, cos_t, sin_t, xi_t, zeta_t, decay, cdec, gn)


def _out_proj_kernel(yl_ref, yr_ref, x_ref, w_ref, g_ref, x1_ref, h_ref):
    acc = jnp.dot(yl_ref[...], w_ref[0:D_RNN, :], preferred_element_type=F32)
    acc = acc + jnp.dot(yr_ref[...], w_ref[D_RNN:D_MODEL, :], preferred_element_type=F32)
    x1 = x_ref[...] + acc
    x1_ref[...] = x1
    h_ref[...] = _rms(x1, g_ref[...]).astype(BF16)


def _out_proj_router_kernel(yl_ref, yr_ref, x_ref, w_ref, g_ref, rw_ref, x1_ref, idx_ref, gate_ref):
    acc = jnp.dot(yl_ref[...], w_ref[0:D_RNN, :], preferred_element_type=F32)
    acc = acc + jnp.dot(yr_ref[...], w_ref[D_RNN:D_MODEL, :], preferred_element_type=F32)
    x1 = x_ref[...] + acc
    x1_ref[...] = x1
    h = _rms(x1, g_ref[...]).astype(BF16)
    logits = jnp.dot(h, rw_ref[...], preferred_element_type=F32)
    lane_i = lax.broadcasted_iota(jnp.int32, logits.shape, 1)
    lane = lane_i.astype(F32)
    neg = jnp.float32(-jnp.inf)
    logits = jnp.where(lane_i < N_EXPERTS, logits, neg)
    m1 = jnp.max(logits, axis=-1, keepdims=True)
    i1 = jnp.min(jnp.where(logits == m1, lane, float(LANES)), axis=-1, keepdims=True)
    rest = jnp.where(lane == i1, neg, logits)
    m2 = jnp.max(rest, axis=-1, keepdims=True)
    i2 = jnp.min(jnp.where(rest == m2, lane, float(LANES)), axis=-1, keepdims=True)
    e2 = jnp.exp(m2 - m1)
    den = 1.0 + e2
    idx_ref[...] = jnp.where(lane_i == 0, i1, i2).astype(jnp.int32)
    gate_ref[...] = jnp.where(lane_i == 0, 1.0 / den, e2 / den)


def _out_proj(yl, yr, x, w, g, router=None):
    t, d = x.shape
    tm = min(OUT_TM, t)
    row = lambda i: (i, 0)
    fixed = lambda i: (0, 0)
    in_specs = [
        pl.BlockSpec((tm, D_RNN), row),
        pl.BlockSpec((tm, D_RET), row),
        pl.BlockSpec((tm, d), row),
        pl.BlockSpec((d, d), fixed),
        pl.BlockSpec((1, d), fixed),
    ]
    if router is None:
        return pl.pallas_call(
            _out_proj_kernel,
            grid=(t // tm,),
            in_specs=in_specs,
            out_specs=[pl.BlockSpec((tm, d), row), pl.BlockSpec((tm, d), row)],
            out_shape=[jax.ShapeDtypeStruct((t, d), F32), jax.ShapeDtypeStruct((t, d), BF16)],
            compiler_params=_cparams(("parallel",)),
            name="out_proj_norm",
        )(yl, yr, x, w, g)
    return pl.pallas_call(
        _out_proj_router_kernel,
        grid=(t // tm,),
        in_specs=in_specs + [pl.BlockSpec((d, LANES), fixed)],
        out_specs=[pl.BlockSpec((tm, d), row), pl.BlockSpec((tm, LANES), row),
                   pl.BlockSpec((tm, LANES), row)],
        out_shape=[jax.ShapeDtypeStruct((t, d), F32), jax.ShapeDtypeStruct((t, LANES), jnp.int32),
                   jax.ShapeDtypeStruct((t, LANES), F32)],
        compiler_params=_cparams(("parallel",)),
        name="out_proj_router",
    )(yl, yr, x, w, g, router)


def _swiglu_step(h, wg, wu, wd):
    gt = jnp.dot(h, wg, preferred_element_type=F32)
    up = jnp.dot(h, wu, preferred_element_type=F32)
    act = (jax.nn.silu(gt) * up).astype(BF16)
    return jnp.dot(act, wd, preferred_element_type=F32)


def _ffn_kernel(h_ref, x_ref, wg_ref, wu_ref, wd_ref, o_ref):
    @pl.when(pl.program_id(1) == 0)
    def _():
        o_ref[...] = x_ref[...]

    o_ref[...] += _swiglu_step(h_ref[...], wg_ref[...], wu_ref[...], wd_ref[...])


def _dense_ffn(h, x, wg, wu, wd):
    t, d = x.shape
    tm = min(FFN_TM, t)
    tf = FFN_TF
    return pl.pallas_call(
        _ffn_kernel,
        grid=(t // tm, D_FF // tf),
        in_specs=[
            pl.BlockSpec((tm, d), lambda i, f: (i, 0)),
            pl.BlockSpec((tm, d), lambda i, f: (i, 0)),
            pl.BlockSpec((d, tf), lambda i, f: (0, f)),
            pl.BlockSpec((d, tf), lambda i, f: (0, f)),
            pl.BlockSpec((tf, d), lambda i, f: (f, 0)),
        ],
        out_specs=pl.BlockSpec((tm, d), lambda i, f: (i, 0)),
        out_shape=jax.ShapeDtypeStruct((t, d), F32),
        compiler_params=_cparams(("parallel", "arbitrary")),
        name="dense_swiglu",
    )(h, x, wg, wu, wd)


def _moe_kernel(te_ref, tv_ref, src_ref, nxt_ref, x_hbm, g_ref, wg_ref, wu_ref, wd_ref, o_ref,
                xg, hs, sem):
    i = pl.program_id(0)
    f = pl.program_id(1)
    tm = xg.shape[1]
    slot = i % 2

    def start_rows(idx_ref, dst_slot):
        def body(r, carry):
            tok = idx_ref[0, 0, r]
            pltpu.make_async_copy(x_hbm.at[pl.ds(tok, 1)], xg.at[dst_slot, pl.ds(r, 1)],
                                  sem.at[dst_slot]).start()
            return carry

        lax.fori_loop(0, tm, body, 0, unroll=8)

    @pl.when((i == 0) & (f == 0))
    def _():
        start_rows(src_ref, 0)

    @pl.when(f == 0)
    def _():
        pltpu.make_async_copy(x_hbm.at[pl.ds(0, tm)], xg.at[slot], sem.at[slot]).wait()

        @pl.when(i + 1 < pl.num_programs(0))
        def _():
            start_rows(nxt_ref, 1 - slot)

        hs[...] = _rms(xg[slot], g_ref[...]).astype(BF16)
        o_ref[...] = jnp.zeros(o_ref.shape, F32)

    @pl.when(tv_ref[i] == 1)
    def _():
        o_ref[...] += _swiglu_step(hs[...], wg_ref[0], wu_ref[0], wd_ref[0])


def _moe_ffn(x, g, tile_expert, tile_valid, src3, wg, wu, wd):
    t, d = x.shape
    n_tiles, _, tm = src3.shape
    tf = FFN_TF
    nf = D_FF // tf

    def w_col(i, f, te, tv):
        return (te[i], 0, jnp.where(tv[i] == 1, f, nf - 1))

    def w_row(i, f, te, tv):
        return (te[i], jnp.where(tv[i] == 1, f, nf - 1), 0)

    grid_spec = pltpu.PrefetchScalarGridSpec(
        num_scalar_prefetch=2,
        grid=(n_tiles, nf),
        in_specs=[
            pl.BlockSpec((1, 1, tm), lambda i, f, te, tv: (i, 0, 0), memory_space=pltpu.SMEM),
            pl.BlockSpec((1, 1, tm), lambda i, f, te, tv: (jnp.minimum(i + 1, n_tiles - 1), 0, 0),
                         memory_space=pltpu.SMEM),
            pl.BlockSpec(memory_space=pl.ANY),
            pl.BlockSpec((1, d), lambda i, f, te, tv: (0, 0)),
            pl.BlockSpec((1, d, tf), w_col),
            pl.BlockSpec((1, d, tf), w_col),
            pl.BlockSpec((1, tf, d), w_row),
        ],
        out_specs=pl.BlockSpec((tm, d), lambda i, f, te, tv: (i, 0)),
        scratch_shapes=[
            pltpu.VMEM((2, tm, d), F32),
            pltpu.VMEM((tm, d), BF16),
            pltpu.SemaphoreType.DMA((2,)),
        ],
    )
    return pl.pallas_call(
        _moe_kernel,
        grid_spec=grid_spec,
        out_shape=jax.ShapeDtypeStruct((n_tiles * tm, d), F32),
        compiler_params=_cparams(("arbitrary", "arbitrary")),
        name="expert_swiglu",
    )(tile_expert, tile_valid, src3, src3, x, g, wg, wu, wd)


def _combine_kernel(pos_ref, nxt_ref, x_ref, gate_ref, g_ref, ys_hbm, o_ref, buf, sem):
    i = pl.program_id(0)
    tt = x_ref.shape[0]
    slot = i % 2

    def start_rows(idx_ref, dst_slot):
        def body(r, carry):
            for k in range(TOP_K):
                p = idx_ref[0, 0, TOP_K * r + k]
                pltpu.make_async_copy(ys_hbm.at[pl.ds(p, 1)], buf.at[dst_slot, k, pl.ds(r, 1)],
                                      sem.at[dst_slot]).start()
            return carry

        lax.fori_loop(0, tt, body, 0, unroll=4)

    @pl.when(i == 0)
    def _():
        start_rows(pos_ref, 0)

    for k in range(TOP_K):
        pltpu.make_async_copy(ys_hbm.at[pl.ds(0, tt)], buf.at[slot, k], sem.at[slot]).wait()

    @pl.when(i + 1 < pl.num_programs(0))
    def _():
        start_rows(nxt_ref, 1 - slot)

    gates = gate_ref[...]
    moe = gates[:, 0:1] * buf[slot, 0]
    for k in range(1, TOP_K):
        moe = moe + gates[:, k:k + 1] * buf[slot, k]
    o_ref[...] = _rms(x_ref[...] + moe, g_ref[...])


def _combine(x, gates, pos3, ys, g):
    t, d = x.shape
    n_steps, _, per = pos3.shape
    tt = per // TOP_K
    return pl.pallas_call(
        _combine_kernel,
        grid=(n_steps,),
        in_specs=[
            pl.BlockSpec((1, 1, per), lambda i: (i, 0, 0), memory_space=pltpu.SMEM),
            pl.BlockSpec((1, 1, per), lambda i: (jnp.minimum(i + 1, n_steps - 1), 0, 0),
                         memory_space=pltpu.SMEM),
            pl.BlockSpec((tt, d), lambda i: (i, 0)),
            pl.BlockSpec((tt, LANES), lambda i: (i, 0)),
            pl.BlockSpec((1, d), lambda i: (0, 0)),
            pl.BlockSpec(memory_space=pl.ANY),
        ],
        out_specs=pl.BlockSpec((tt, d), lambda i: (i, 0)),
        out_shape=jax.ShapeDtypeStruct((t, d), F32),
        scratch_shapes=[pltpu.VMEM((2, TOP_K, tt, d), F32), pltpu.SemaphoreType.DMA((2,))],
        compiler_params=_cparams(("arbitrary",)),
        name="combine_final_norm",
    )(pos3, pos3, x, gates, g, ys)


def _routing_plan(top_idx, tm):
    t = top_idx.shape[0]
    n_slots = t * TOP_K
    e_flat = top_idx.reshape(n_slots)
    onehot = (e_flat[:, None] == jnp.arange(N_EXPERTS, dtype=jnp.int32)[None, :]).astype(jnp.int32)
    csum = jnp.cumsum(onehot, axis=0)
    counts = csum[-1]
    rank = jnp.sum((csum - onehot) * onehot, axis=1)
    padded = ((counts + tm - 1) // tm) * tm
    ends = jnp.cumsum(padded)
    starts = ends - padded
    pos = jnp.sum(starts[None, :] * onehot, axis=1) + rank
    n_tiles = n_slots // tm + N_EXPERTS
    src = jnp.zeros((n_tiles * tm,), jnp.int32).at[pos].set(
        jnp.arange(n_slots, dtype=jnp.int32) // TOP_K)
    tile_start = jnp.arange(n_tiles, dtype=jnp.int32) * tm
    tile_expert = jnp.minimum(
        jnp.sum((tile_start[:, None] >= ends[None, :]).astype(jnp.int32), axis=1), N_EXPERTS - 1)
    tile_valid = (tile_start < ends[-1]).astype(jnp.int32)
    return pos, src.reshape(n_tiles, 1, tm), tile_expert, tile_valid


def kernel(x, norm1_g, w_in, conv_w, conv_b, lru_wa, lru_ba, lru_wx, lru_bx, lru_lambda,
           lru_norm_g, ret_norm_g, w_out, norm2_g, ffn_w_gate, ffn_w_up, ffn_w_down,
           moe_router, moe_w_gate, moe_w_up, moe_w_down, final_norm_g):
    b, s, d = x.shape
    t = b * s
    tables = _ret_tables(s)
    xf = x.reshape(t, d)
    row = lambda v: v.reshape(1, -1)
    out = None
    for layer in range(DEPTH):
        proj = _norm_matmul(xf, row(norm1_g[layer]), w_in[layer].astype(BF16))
        proj3 = proj.reshape(b, s, D_IN)
        y_lru = _lru_branch(proj3, conv_w[layer], row(conv_b[layer]),
                            lru_wa[layer].astype(BF16), row(lru_ba[layer]),
                            lru_wx[layer].astype(BF16), row(lru_bx[layer]),
                            row(lru_lambda[layer]), row(lru_norm_g[layer]))
        y_ret = _ret_branch(proj3, tables, row(ret_norm_g[layer]))
        yl = y_lru.reshape(t, D_RNN)
        yr = y_ret.reshape(t, D_RET)
        w_o = w_out[layer].astype(BF16)
        j = layer // 2
        if layer % 2 == 0:
            x1, h2 = _out_proj(yl, yr, xf, w_o, row(norm2_g[layer]))
            xf = _dense_ffn(h2, x1, ffn_w_gate[j].astype(BF16), ffn_w_up[j].astype(BF16),
                            ffn_w_down[j].astype(BF16))
        else:
            router = jnp.pad(moe_router[j], ((0, 0), (0, LANES - N_EXPERTS))).astype(BF16)
            x1, idx, gates = _out_proj(yl, yr, xf, w_o, row(norm2_g[layer]), router)
            tm = min(FFN_TM, t)
            pos, src3, tile_expert, tile_valid = _routing_plan(idx[:, :TOP_K], tm)
            ys = _moe_ffn(x1, row(norm2_g[layer]), tile_expert, tile_valid, src3,
                          moe_w_gate[j].astype(BF16), moe_w_up[j].astype(BF16),
                          moe_w_down[j].astype(BF16))
            tt = min(COMB_TT, t)
            pos3 = pos.reshape(t // tt, 1, tt * TOP_K)
            out = _combine(x1, gates, pos3, ys, row(final_norm_g))
    return out.reshape(b, s, d)
```

```python
import functools

import jax
import jax.numpy as jnp
from jax import lax
from jax.experimental import pallas as pl
from jax.experimental.pallas import tpu as pltpu

F32 = jnp.float32
BF16 = jnp.bfloat16

D_MODEL = 2048
DEPTH = 2
D_RNN = D_MODEL // 2
D_RET = D_MODEL - D_RNN
LRU_BLOCKS = 8
LRU_BLOCK = D_RNN // LRU_BLOCKS
CONV_WIDTH = 4
LRU_C = 8.0
RET_HEADS = 8
RET_DV = D_RET // RET_HEADS
RET_DK = RET_DV // 2
RET_CHUNK = 128
ROPE_BASE = 10000.0
D_FF = 256 * ((8 * D_MODEL // 3 + 255) // 256)
N_EXPERTS = 8
TOP_K = 2
EPS = 1e-6
D_QK = RET_HEADS * RET_DK
D_IN = 2 * D_RNN + 2 * D_QK + 2 * D_RET

LANES = 128
SUBLANES = 8
VMEM_LIMIT_BYTES = 56 * 1024 * 1024

PROJ_TM = 1024
PROJ_TN = 1024
SEQ_TS = 256
OUT_TM = 512
FFN_TM = 512
FFN_TF = 512
COMB_TT = 256


def _cparams(sem):
    return pltpu.CompilerParams(dimension_semantics=sem, vmem_limit_bytes=VMEM_LIMIT_BYTES)


def _rms(x, g):
    return x * lax.rsqrt(jnp.mean(x * x, axis=-1, keepdims=True) + EPS) * g


def _with_casts(body, n_in, n_out, n_cast):
    def kern(*refs):
        ins = refs[:n_in]
        cast_in = refs[n_in:n_in + n_cast]
        outs = refs[n_in + n_cast:n_in + n_cast + n_out]
        cast_out = refs[n_in + n_cast + n_out:n_in + 2 * n_cast + n_out]
        scratch = refs[n_in + 2 * n_cast + n_out:]
        for src, dst in zip(cast_in, cast_out):
            dst[...] = src[...].astype(BF16)
        body(*ins, *outs, *scratch)

    return kern


def _cast_specs(casts, n_steps, inner_steps):
    specs, shapes = [], []
    for a in casts:
        rows, cols = a.shape
        specs.append(pl.BlockSpec((rows // n_steps, cols),
                                  lambda i, j: (i * inner_steps + j, 0)))
        shapes.append(jax.ShapeDtypeStruct((rows, cols), BF16))
    return specs, shapes


def _norm_matmul_kernel(x_ref, g_ref, w_ref, o_ref, h_ref):
    @pl.when(pl.program_id(1) == 0)
    def _():
        h_ref[...] = _rms(x_ref[...], g_ref[...]).astype(BF16)

    o_ref[...] = jnp.dot(h_ref[...], w_ref[...], preferred_element_type=F32)


def _norm_matmul(x, g, w):
    t, d = x.shape
    n = w.shape[1]
    tm = min(PROJ_TM, t)
    tn = PROJ_TN
    return pl.pallas_call(
        _norm_matmul_kernel,
        grid=(t // tm, n // tn),
        in_specs=[
            pl.BlockSpec((tm, d), lambda i, j: (i, 0)),
            pl.BlockSpec((1, d), lambda i, j: (0, 0)),
            pl.BlockSpec((d, tn), lambda i, j: (0, j)),
        ],
        out_specs=pl.BlockSpec((tm, tn), lambda i, j: (i, j)),
        out_shape=jax.ShapeDtypeStruct((t, n), F32),
        scratch_shapes=[pltpu.VMEM((tm, d), BF16)],
        compiler_params=_cparams(("parallel", "arbitrary")),
        name="norm_in_proj",
    )(x, g, w)


def _lru_kernel(x_ref, gate_ref, cw_ref, cb_ref, wa_ref, ba_ref, wx_ref, bx_ref, lam_ref,
                ng_ref, o_ref, xbuf, a_s, b_s, hcar):
    c = pl.program_id(1)
    ts = x_ref.shape[1]
    hist = SUBLANES

    @pl.when(c == 0)
    def _():
        xbuf[0:hist, :] = jnp.zeros((hist, D_RNN), F32)
        hcar[...] = jnp.zeros((1, D_RNN), F32)

    x = x_ref[0]
    xbuf[hist:hist + ts, :] = x
    xc = cb_ref[...]
    for tap in range(CONV_WIDTH):
        off = hist - (CONV_WIDTH - 1) + tap
        xc = xc + xbuf[off:off + ts, :] * cw_ref[tap:tap + 1, :]
    xbuf[0:hist, :] = x[ts - hist:ts, :]

    xcb = xc.astype(BF16)
    ra = []
    rx = []
    for n in range(LRU_BLOCKS):
        xs = xcb[:, n * LRU_BLOCK:(n + 1) * LRU_BLOCK]
        ra.append(jnp.dot(xs, wa_ref[n], preferred_element_type=F32))
        rx.append(jnp.dot(xs, wx_ref[n], preferred_element_type=F32))
    r = jax.nn.sigmoid(jnp.concatenate(ra, axis=1) + ba_ref[...])
    ig = jax.nn.sigmoid(jnp.concatenate(rx, axis=1) + bx_ref[...])

    z = -lam_ref[...]
    softplus = jnp.maximum(z, 0.0) + jnp.log1p(jnp.exp(-jnp.abs(z)))
    log_a = (-LRU_C * r) * softplus
    a = jnp.exp(log_a)
    mult = jnp.sqrt(1.0 - a * a)
    a_s[...] = a
    b_s[...] = mult * ig * xc

    @pl.when(c == 0)
    def _():
        b_s[0:1, :] = ig[0:1, :] * xc[0:1, :]

    rows8 = lax.broadcasted_iota(jnp.int32, (SUBLANES, D_RNN), 0)

    def body(i, h):
        r0 = pl.multiple_of(i * SUBLANES, SUBLANES)
        av = a_s[pl.ds(r0, SUBLANES), :]
        bv = b_s[pl.ds(r0, SUBLANES), :]
        for sh in (1, 2, 4):
            a_sh = pltpu.roll(av, sh, axis=0)
            b_sh = pltpu.roll(bv, sh, axis=0)
            m = rows8 >= sh
            bv = jnp.where(m, av * b_sh + bv, bv)
            av = jnp.where(m, av * a_sh, av)
        hv = av * h + bv
        b_s[pl.ds(r0, SUBLANES), :] = hv
        return hv[SUBLANES - 1:SUBLANES, :]

    hcar[...] = lax.fori_loop(0, ts // SUBLANES, body, hcar[...])

    y = _rms(b_s[...], ng_ref[...])
    o_ref[0] = (y * jax.nn.gelu(gate_ref[0])).astype(BF16)


def _lru_branch(proj3, cw, cb, wa, ba, wx, bx, lam, ng, casts):
    b, s, _ = proj3.shape
    ts = min(SEQ_TS, s)
    vec = pl.BlockSpec((1, D_RNN), lambda i, j: (0, 0))
    blk = pl.BlockSpec((LRU_BLOCKS, LRU_BLOCK, LRU_BLOCK), lambda i, j: (0, 0, 0))
    in_specs = [
        pl.BlockSpec((1, ts, D_RNN), lambda i, j: (i, j, 0)),
        pl.BlockSpec((1, ts, D_RNN), lambda i, j: (i, j, 1)),
        pl.BlockSpec((CONV_WIDTH, D_RNN), lambda i, j: (0, 0)),
        vec, blk, vec, blk, vec, vec, vec,
    ]
    cast_specs, cast_shapes = _cast_specs(casts, b * (s // ts), s // ts)
    outs = pl.pallas_call(
        _with_casts(_lru_kernel, len(in_specs), 1, len(casts)),
        grid=(b, s // ts),
        in_specs=in_specs + cast_specs,
        out_specs=[pl.BlockSpec((1, ts, D_RNN), lambda i, j: (i, j, 0))] + cast_specs,
        out_shape=[jax.ShapeDtypeStruct((b, s, D_RNN), BF16)] + cast_shapes,
        scratch_shapes=[
            pltpu.VMEM((ts + SUBLANES, D_RNN), F32),
            pltpu.VMEM((ts, D_RNN), F32),
            pltpu.VMEM((ts, D_RNN), F32),
            pltpu.VMEM((1, D_RNN), F32),
        ],
        compiler_params=_cparams(("arbitrary", "arbitrary")),
        name="rglru_branch",
    )(proj3, proj3, cw, cb, wa, ba, wx, bx, lam, ng, *casts)
    return outs[0], outs[1:]


def _rope(x, cos, sin_signed):
    half = RET_DK // 2
    width = x.shape[1]
    lane = lax.broadcasted_iota(jnp.int32, x.shape, 1)
    first = (lane & (RET_DK - 1)) < half
    rot = jnp.where(first, pltpu.roll(x, width - half, axis=1), pltpu.roll(x, half, axis=1))
    return x * cos + rot * sin_signed


def _ret_kernel(q_ref, k_ref, v_ref, g_ref, cos_ref, sin_ref, xi_ref, zeta_ref, decay_ref,
                cdec_ref, gn_ref, o_ref, state):
    c = pl.program_id(1)
    ts = q_ref.shape[1]

    @pl.when(c == 0)
    def _():
        state[...] = jnp.zeros(state.shape, F32)

    cos = cos_ref[...]
    sin = sin_ref[...]
    q = _rope(q_ref[0], cos, sin)
    k = _rope(k_ref[0], cos, sin) * (RET_DK ** -0.5)
    for n in range(ts // RET_CHUNK):
        lo = n * RET_CHUNK
        qn = q[lo:lo + RET_CHUNK, :]
        kn = k[lo:lo + RET_CHUNK, :]
        q_in = qn.astype(BF16)
        k_in = kn.astype(BF16)
        q_x = (qn * xi_ref[...]).astype(BF16)
        k_z = (kn * zeta_ref[...]).astype(BF16)
        for h in range(RET_HEADS):
            ks = slice(h * RET_DK, (h + 1) * RET_DK)
            vs = slice(h * RET_DV, (h + 1) * RET_DV)
            vh = v_ref[0, lo:lo + RET_CHUNK, vs].astype(BF16)
            scores = lax.dot_general(q_in[:, ks], k_in[:, ks], (((1,), (1,)), ((), ())),
                                     preferred_element_type=F32) * decay_ref[h]
            inner = jnp.dot(scores.astype(BF16), vh, preferred_element_type=F32)
            st = state[h]
            cross = jnp.dot(q_x[:, ks], st.astype(BF16), preferred_element_type=F32)
            kv = lax.dot_general(k_z[:, ks], vh, (((0,), (0,)), ((), ())),
                                 preferred_element_type=F32)
            state[h] = cdec_ref[h] * st + kv
            o = inner + cross
            mu = jnp.mean(o, axis=-1, keepdims=True)
            d = o - mu
            var = jnp.mean(d * d, axis=-1, keepdims=True)
            y = d * lax.rsqrt(var + EPS) * gn_ref[:, vs]
            gate = g_ref[0, lo:lo + RET_CHUNK, vs]
            o_ref[0, lo:lo + RET_CHUNK, vs] = (y * jax.nn.silu(gate)).astype(BF16)


def _ret_tables(s):
    inv_freq = ROPE_BASE ** (-jnp.arange(0, RET_DK, 2, dtype=F32) / RET_DK)
    ang = jnp.arange(s, dtype=F32)[:, None] * inv_freq[None, :]
    cos = jnp.cos(ang)
    sin = jnp.sin(ang)
    cos_t = jnp.tile(jnp.concatenate([cos, cos], axis=1), (1, RET_HEADS))
    sin_t = jnp.tile(jnp.concatenate([-sin, sin], axis=1), (1, RET_HEADS))
    log_g = jnp.log(1.0 - 2.0 ** (-5.0 - jnp.arange(RET_HEADS, dtype=F32)))
    idx = jnp.arange(RET_CHUNK, dtype=F32)
    rel = idx[:, None] - idx[None, :]
    decay = jnp.where(rel >= 0, jnp.exp(jnp.maximum(rel, 0.0)[None] * log_g[:, None, None]), 0.0)
    zeta = jnp.exp((RET_CHUNK - 1.0 - idx)[None, :] * log_g[:, None])
    xi = jnp.exp((idx + 1.0)[None, :] * log_g[:, None])
    zeta_t = jnp.repeat(zeta.T, RET_DK, axis=1)
    xi_t = jnp.repeat(xi.T, RET_DK, axis=1)
    cdec = jnp.broadcast_to(jnp.exp(RET_CHUNK * log_g)[:, None, None], (RET_HEADS, RET_DK, RET_DV))
    return cos_t, sin_t, xi_t, zeta_t, decay, cdec


def _ret_branch(proj3, tables, gn, casts):
    b, s, _ = proj3.shape
    ts = min(SEQ_TS, s)
    cos_t, sin_t, xi_t, zeta_t, decay, cdec = tables
    q_blk = 2 * D_RNN // D_QK
    cast_specs, cast_shapes = _cast_specs(casts, b * (s // ts), s // ts)
    outs = pl.pallas_call(
        _with_casts(_ret_kernel, 11, 1, len(casts)),
        grid=(b, s // ts),
        in_specs=[
            pl.BlockSpec((1, ts, D_QK), lambda i, j: (i, j, q_blk)),
            pl.BlockSpec((1, ts, D_QK), lambda i, j: (i, j, q_blk + 1)),
            pl.BlockSpec((1, ts, D_RET), lambda i, j: (i, j, 3)),
            pl.BlockSpec((1, ts, D_RET), lambda i, j: (i, j, 4)),
            pl.BlockSpec((ts, D_QK), lambda i, j: (j, 0)),
            pl.BlockSpec((ts, D_QK), lambda i, j: (j, 0)),
            pl.BlockSpec((RET_CHUNK, D_QK), lambda i, j: (0, 0)),
            pl.BlockSpec((RET_CHUNK, D_QK), lambda i, j: (0, 0)),
            pl.BlockSpec((RET_HEADS, RET_CHUNK, RET_CHUNK), lambda i, j: (0, 0, 0)),
            pl.BlockSpec((RET_HEADS, RET_DK, RET_DV), lambda i, j: (0, 0, 0)),
            pl.BlockSpec((1, D_RET), lambda i, j: (0, 0)),
        ] + cast_specs,
        out_specs=[pl.BlockSpec((1, ts, D_RET), lambda i, j: (i, j, 0))] + cast_specs,
        out_shape=[jax.ShapeDtypeStruct((b, s, D_RET), BF16)] + cast_shapes,
        scratch_shapes=[pltpu.VMEM((RET_HEADS, RET_DK, RET_DV), F32)],
        compiler_params=_cparams(("arbitrary", "arbitrary")),
        name="retention_branch",
    )(proj3, proj3, proj3, proj3, cos_t, sin_t, xi_t, zeta_t, decay, cdec, gn, *casts)
    return outs[0], outs[1:]


def _out_proj_kernel(yl_ref, yr_ref, x_ref, w_ref, g_ref, x1_ref, h_ref):
    acc = jnp.dot(yl_ref[...], w_ref[0:D_RNN, :], preferred_element_type=F32)
    acc = acc + jnp.dot(yr_ref[...], w_ref[D_RNN:D_MODEL, :], preferred_element_type=F32)
    x1 = x_ref[...] + acc
    x1_ref[...] = x1
    h_ref[...] = _rms(x1, g_ref[...]).astype(BF16)


def _out_proj_router_kernel(yl_ref, yr_ref, x_ref, w_ref, g_ref, rw_ref, x1_ref, idx_ref, gate_ref):
    acc = jnp.dot(yl_ref[...], w_ref[0:D_RNN, :], preferred_element_type=F32)
    acc = acc + jnp.dot(yr_ref[...], w_ref[D_RNN:D_MODEL, :], preferred_element_type=F32)
    x1 = x_ref[...] + acc
    x1_ref[...] = x1
    h = _rms(x1, g_ref[...]).astype(BF16)
    logits = jnp.dot(h, rw_ref[...], preferred_element_type=F32)
    lane_i = lax.broadcasted_iota(jnp.int32, logits.shape, 1)
    lane = lane_i.astype(F32)
    neg = jnp.float32(-jnp.inf)
    logits = jnp.where(lane_i < N_EXPERTS, logits, neg)
    m1 = jnp.max(logits, axis=-1, keepdims=True)
    i1 = jnp.min(jnp.where(logits == m1, lane, float(LANES)), axis=-1, keepdims=True)
    rest = jnp.where(lane == i1, neg, logits)
    m2 = jnp.max(rest, axis=-1, keepdims=True)
    i2 = jnp.min(jnp.where(rest == m2, lane, float(LANES)), axis=-1, keepdims=True)
    e2 = jnp.exp(m2 - m1)
    den = 1.0 + e2
    idx_ref[...] = jnp.where(lane_i == 0, i1, i2).astype(jnp.int32)
    gate_ref[...] = jnp.where(lane_i == 0, 1.0 / den, e2 / den)


def _out_proj(yl, yr, x, w, g, router=None):
    t, d = x.shape
    tm = min(OUT_TM, t)
    row = lambda i: (i, 0)
    fixed = lambda i: (0, 0)
    in_specs = [
        pl.BlockSpec((tm, D_RNN), row),
        pl.BlockSpec((tm, D_RET), row),
        pl.BlockSpec((tm, d), row),
        pl.BlockSpec((d, d), fixed),
        pl.BlockSpec((1, d), fixed),
    ]
    if router is None:
        return pl.pallas_call(
            _out_proj_kernel,
            grid=(t // tm,),
            in_specs=in_specs,
            out_specs=[pl.BlockSpec((tm, d), row), pl.BlockSpec((tm, d), row)],
            out_shape=[jax.ShapeDtypeStruct((t, d), F32), jax.ShapeDtypeStruct((t, d), BF16)],
            compiler_params=_cparams(("parallel",)),
            name="out_proj_norm",
        )(yl, yr, x, w, g)
    return pl.pallas_call(
        _out_proj_router_kernel,
        grid=(t // tm,),
        in_specs=in_specs + [pl.BlockSpec((d, LANES), fixed)],
        out_specs=[pl.BlockSpec((tm, d), row), pl.BlockSpec((tm, LANES), row),
                   pl.BlockSpec((tm, LANES), row)],
        out_shape=[jax.ShapeDtypeStruct((t, d), F32), jax.ShapeDtypeStruct((t, LANES), jnp.int32),
                   jax.ShapeDtypeStruct((t, LANES), F32)],
        compiler_params=_cparams(("parallel",)),
        name="out_proj_router",
    )(yl, yr, x, w, g, router)


def _swiglu_step(h, wg, wu, wd):
    gt = jnp.dot(h, wg, preferred_element_type=F32)
    up = jnp.dot(h, wu, preferred_element_type=F32)
    act = (jax.nn.silu(gt) * up).astype(BF16)
    return jnp.dot(act, wd, preferred_element_type=F32)


def _ffn_kernel(h_ref, x_ref, wg_ref, wu_ref, wd_ref, o_ref):
    @pl.when(pl.program_id(1) == 0)
    def _():
        o_ref[...] = x_ref[...]

    o_ref[...] += _swiglu_step(h_ref[...], wg_ref[...], wu_ref[...], wd_ref[...])


def _dense_ffn(h, x, wg, wu, wd):
    t, d = x.shape
    tm = min(FFN_TM, t)
    tf = FFN_TF
    return pl.pallas_call(
        _ffn_kernel,
        grid=(t // tm, D_FF // tf),
        in_specs=[
            pl.BlockSpec((tm, d), lambda i, f: (i, 0)),
            pl.BlockSpec((tm, d), lambda i, f: (i, 0)),
            pl.BlockSpec((d, tf), lambda i, f: (0, f)),
            pl.BlockSpec((d, tf), lambda i, f: (0, f)),
            pl.BlockSpec((tf, d), lambda i, f: (f, 0)),
        ],
        out_specs=pl.BlockSpec((tm, d), lambda i, f: (i, 0)),
        out_shape=jax.ShapeDtypeStruct((t, d), F32),
        compiler_params=_cparams(("parallel", "arbitrary")),
        name="dense_swiglu",
    )(h, x, wg, wu, wd)


def _moe_kernel(te_ref, tv_ref, src_ref, nxt_ref, x_hbm, g_ref, wg_ref, wu_ref, wd_ref, o_ref,
                xg, hs, sem):
    i = pl.program_id(0)
    f = pl.program_id(1)
    tm = xg.shape[1]
    slot = i % 2

    def start_rows(idx_ref, dst_slot):
        def body(r, carry):
            tok = idx_ref[0, 0, r]
            pltpu.make_async_copy(x_hbm.at[pl.ds(tok, 1)], xg.at[dst_slot, pl.ds(r, 1)],
                                  sem.at[dst_slot]).start()
            return carry

        lax.fori_loop(0, tm, body, 0, unroll=8)

    @pl.when((i == 0) & (f == 0))
    def _():
        start_rows(src_ref, 0)

    @pl.when(f == 0)
    def _():
        pltpu.make_async_copy(x_hbm.at[pl.ds(0, tm)], xg.at[slot], sem.at[slot]).wait()

        @pl.when(i + 1 < pl.num_programs(0))
        def _():
            start_rows(nxt_ref, 1 - slot)

        hs[...] = _rms(xg[slot], g_ref[...]).astype(BF16)
        o_ref[...] = jnp.zeros(o_ref.shape, F32)

    @pl.when(tv_ref[i] == 1)
    def _():
        o_ref[...] += _swiglu_step(hs[...], wg_ref[0], wu_ref[0], wd_ref[0])


def _moe_ffn(x, g, tile_expert, tile_valid, src3, wg, wu, wd):
    t, d = x.shape
    n_tiles, _, tm = src3.shape
    tf = FFN_TF
    nf = D_FF // tf

    def w_col(i, f, te, tv):
        return (te[i], 0, jnp.where(tv[i] == 1, f, nf - 1))

    def w_row(i, f, te, tv):
        return (te[i], jnp.where(tv[i] == 1, f, nf - 1), 0)

    grid_spec = pltpu.PrefetchScalarGridSpec(
        num_scalar_prefetch=2,
        grid=(n_tiles, nf),
        in_specs=[
            pl.BlockSpec((1, 1, tm), lambda i, f, te, tv: (i, 0, 0), memory_space=pltpu.SMEM),
            pl.BlockSpec((1, 1, tm), lambda i, f, te, tv: (jnp.minimum(i + 1, n_tiles - 1), 0, 0),
                         memory_space=pltpu.SMEM),
            pl.BlockSpec(memory_space=pl.ANY),
            pl.BlockSpec((1, d), lambda i, f, te, tv: (0, 0)),
            pl.BlockSpec((1, d, tf), w_col),
            pl.BlockSpec((1, d, tf), w_col),
            pl.BlockSpec((1, tf, d), w_row),
        ],
        out_specs=pl.BlockSpec((tm, d), lambda i, f, te, tv: (i, 0)),
        scratch_shapes=[
            pltpu.VMEM((2, tm, d), F32),
            pltpu.VMEM((tm, d), BF16),
            pltpu.SemaphoreType.DMA((2,)),
        ],
    )
    return pl.pallas_call(
        _moe_kernel,
        grid_spec=grid_spec,
        out_shape=jax.ShapeDtypeStruct((n_tiles * tm, d), F32),
        compiler_params=_cparams(("arbitrary", "arbitrary")),
        name="expert_swiglu",
    )(tile_expert, tile_valid, src3, src3, x, g, wg, wu, wd)


def _combine_kernel(pos_ref, nxt_ref, x_ref, gate_ref, g_ref, ys_hbm, o_ref, buf, sem):
    i = pl.program_id(0)
    tt = x_ref.shape[0]
    slot = i % 2

    def start_rows(idx_ref, dst_slot):
        def body(r, carry):
            for k in range(TOP_K):
                p = idx_ref[0, 0, TOP_K * r + k]
                pltpu.make_async_copy(ys_hbm.at[pl.ds(p, 1)], buf.at[dst_slot, k, pl.ds(r, 1)],
                                      sem.at[dst_slot]).start()
            return carry

        lax.fori_loop(0, tt, body, 0, unroll=4)

    @pl.when(i == 0)
    def _():
        start_rows(pos_ref, 0)

    for k in range(TOP_K):
        pltpu.make_async_copy(ys_hbm.at[pl.ds(0, tt)], buf.at[slot, k], sem.at[slot]).wait()

    @pl.when(i + 1 < pl.num_programs(0))
    def _():
        start_rows(nxt_ref, 1 - slot)

    gates = gate_ref[...]
    moe = gates[:, 0:1] * buf[slot, 0]
    for k in range(1, TOP_K):
        moe = moe + gates[:, k:k + 1] * buf[slot, k]
    o_ref[...] = _rms(x_ref[...] + moe, g_ref[...])


def _combine(x, gates, pos3, ys, g):
    t, d = x.shape
    n_steps, _, per = pos3.shape
    tt = per // TOP_K
    return pl.pallas_call(
        _combine_kernel,
        grid=(n_steps,),
        in_specs=[
            pl.BlockSpec((1, 1, per), lambda i: (i, 0, 0), memory_space=pltpu.SMEM),
            pl.BlockSpec((1, 1, per), lambda i: (jnp.minimum(i + 1, n_steps - 1), 0, 0),
                         memory_space=pltpu.SMEM),
            pl.BlockSpec((tt, d), lambda i: (i, 0)),
            pl.BlockSpec((tt, LANES), lambda i: (i, 0)),
            pl.BlockSpec((1, d), lambda i: (0, 0)),
            pl.BlockSpec(memory_space=pl.ANY),
        ],
        out_specs=pl.BlockSpec((tt, d), lambda i: (i, 0)),
        out_shape=jax.ShapeDtypeStruct((t, d), F32),
        scratch_shapes=[pltpu.VMEM((2, TOP_K, tt, d), F32), pltpu.SemaphoreType.DMA((2,))],
        compiler_params=_cparams(("arbitrary",)),
        name="combine_final_norm",
    )(pos3, pos3, x, gates, g, ys)


def _routing_plan(top_idx, tm):
    t = top_idx.shape[0]
    n_slots = t * TOP_K
    e_flat = top_idx.reshape(n_slots)
    onehot = (e_flat[:, None] == jnp.arange(N_EXPERTS, dtype=jnp.int32)[None, :]).astype(jnp.int32)
    csum = jnp.cumsum(onehot, axis=0)
    counts = csum[-1]
    rank = jnp.sum((csum - onehot) * onehot, axis=1)
    padded = ((counts + tm - 1) // tm) * tm
    ends = jnp.cumsum(padded)
    starts = ends - padded
    pos = jnp.sum(starts[None, :] * onehot, axis=1) + rank
    n_tiles = n_slots // tm + N_EXPERTS
    src = jnp.zeros((n_tiles * tm,), jnp.int32).at[pos].set(
        jnp.arange(n_slots, dtype=jnp.int32) // TOP_K)
    tile_start = jnp.arange(n_tiles, dtype=jnp.int32) * tm
    tile_expert = jnp.minimum(
        jnp.sum((tile_start[:, None] >= ends[None, :]).astype(jnp.int32), axis=1), N_EXPERTS - 1)
    tile_valid = (tile_start < ends[-1]).astype(jnp.int32)
    return pos, src.reshape(n_tiles, 1, tm), tile_expert, tile_valid


def kernel(x, norm1_g, w_in, conv_w, conv_b, lru_wa, lru_ba, lru_wx, lru_bx, lru_lambda,
           lru_norm_g, ret_norm_g, w_out, norm2_g, ffn_w_gate, ffn_w_up, ffn_w_down,
           moe_router, moe_w_gate, moe_w_up, moe_w_down, final_norm_g):
    assert DEPTH == 2 and w_in.shape[0] == DEPTH
    b, s, d = x.shape
    t = b * s
    tables = _ret_tables(s)
    xf = x.reshape(t, d)
    row = lambda v: v.reshape(1, -1)

    def branches(layer, x_in, w_in_bf16, lru_casts, ret_casts):
        proj = _norm_matmul(x_in, row(norm1_g[layer]), w_in_bf16)
        proj3 = proj.reshape(b, s, D_IN)
        y_lru, lru_cast = _lru_branch(proj3, conv_w[layer], row(conv_b[layer]),
                                      lru_wa[layer].astype(BF16), row(lru_ba[layer]),
                                      lru_wx[layer].astype(BF16), row(lru_bx[layer]),
                                      row(lru_lambda[layer]), row(lru_norm_g[layer]), lru_casts)
        y_ret, ret_cast = _ret_branch(proj3, tables, row(ret_norm_g[layer]), ret_casts)
        return y_lru.reshape(t, D_RNN), y_ret.reshape(t, D_RET), lru_cast, ret_cast

    flat2 = lambda w: w.reshape(-1, w.shape[-1])
    yl, yr, (wg0, wu0, wd0, wo0, wi1, wo1), (moe_g,) = branches(
        0, xf, w_in[0].astype(BF16),
        [ffn_w_gate[0], ffn_w_up[0], ffn_w_down[0], w_out[0], w_in[1], w_out[1]],
        [flat2(moe_w_gate[0])])
    x1, h2 = _out_proj(yl, yr, xf, wo0, row(norm2_g[0]))
    xf = _dense_ffn(h2, x1, wg0, wu0, wd0)
    yl, yr, (moe_u,), (moe_d,) = branches(1, xf, wi1, [flat2(moe_w_up[0])],
                                          [flat2(moe_w_down[0])])
    router = jnp.pad(moe_router[0], ((0, 0), (0, LANES - N_EXPERTS))).astype(BF16)
    x1, idx, gates = _out_proj(yl, yr, xf, wo1, row(norm2_g[1]), router)
    tm = min(FFN_TM, t)
    pos, src3, tile_expert, tile_valid = _routing_plan(idx[:, :TOP_K], tm)
    ys = _moe_ffn(x1, row(norm2_g[1]), tile_expert, tile_valid, src3,
                  moe_g.reshape(moe_w_gate.shape[1:]), moe_u.reshape(moe_w_up.shape[1:]),
                  moe_d.reshape(moe_w_down.shape[1:]))
    tt = min(COMB_TT, t)
    pos3 = pos.reshape(t // tt, 1, tt * TOP_K)
    out = _combine(x1, gates, pos3, ys, row(final_norm_g))
    return out.reshape(b, s, d)
```

```python
import functools

import jax
import jax.numpy as jnp
from jax import lax
from jax.experimental import pallas as pl
from jax.experimental.pallas import tpu as pltpu

F32 = jnp.float32
BF16 = jnp.bfloat16

D_MODEL = 2048
DEPTH = 2
D_RNN = D_MODEL // 2
D_RET = D_MODEL - D_RNN
LRU_BLOCKS = 8
LRU_BLOCK = D_RNN // LRU_BLOCKS
CONV_WIDTH = 4
LRU_C = 8.0
RET_HEADS = 8
RET_DV = D_RET // RET_HEADS
RET_DK = RET_DV // 2
RET_CHUNK = 128
ROPE_BASE = 10000.0
D_FF = 256 * ((8 * D_MODEL // 3 + 255) // 256)
N_EXPERTS = 8
TOP_K = 2
EPS = 1e-6
LOG2_E = 1.4426950408889634
D_QK = RET_HEADS * RET_DK
D_IN = 2 * D_RNN + 2 * D_QK + 2 * D_RET

LANES = 128
SUBLANES = 8
VMEM_LIMIT_BYTES = 56 * 1024 * 1024

PROJ_TM = 1024
PROJ_TN = 1024
SEQ_TS = 256
OUT_TM = 512
FFN_TM = 512
FFN_TF = 512
COMB_TT = 256


def _cparams(sem):
    return pltpu.CompilerParams(dimension_semantics=sem, vmem_limit_bytes=VMEM_LIMIT_BYTES)


def _rms(x, g):
    return x * lax.rsqrt(jnp.mean(x * x, axis=-1, keepdims=True) + EPS) * g


def _with_casts(body, n_in, n_out, n_cast):
    def kern(*refs):
        ins = refs[:n_in]
        cast_in = refs[n_in:n_in + n_cast]
        outs = refs[n_in + n_cast:n_in + n_cast + n_out]
        cast_out = refs[n_in + n_cast + n_out:n_in + 2 * n_cast + n_out]
        scratch = refs[n_in + 2 * n_cast + n_out:]
        for src, dst in zip(cast_in, cast_out):
            dst[...] = src[...].astype(BF16)
        body(*ins, *outs, *scratch)

    return kern


def _cast_specs(casts, n_steps, inner_steps):
    specs, shapes = [], []
    for a in casts:
        rows, cols = a.shape
        specs.append(pl.BlockSpec((rows // n_steps, cols),
                                  lambda i, j: (i * inner_steps + j, 0)))
        shapes.append(jax.ShapeDtypeStruct((rows, cols), BF16))
    return specs, shapes


def _norm_matmul_kernel(x_ref, g_ref, w_ref, o_ref, h_ref):
    @pl.when(pl.program_id(1) == 0)
    def _():
        h_ref[...] = _rms(x_ref[...], g_ref[...]).astype(BF16)

    o_ref[...] = jnp.dot(h_ref[...], w_ref[...], preferred_element_type=F32)


def _norm_matmul(x, g, w):
    t, d = x.shape
    n = w.shape[1]
    tm = min(PROJ_TM, t)
    tn = PROJ_TN
    return pl.pallas_call(
        _norm_matmul_kernel,
        grid=(t // tm, n // tn),
        in_specs=[
            pl.BlockSpec((tm, d), lambda i, j: (i, 0)),
            pl.BlockSpec((1, d), lambda i, j: (0, 0)),
            pl.BlockSpec((d, tn), lambda i, j: (0, j)),
        ],
        out_specs=pl.BlockSpec((tm, tn), lambda i, j: (i, j)),
        out_shape=jax.ShapeDtypeStruct((t, n), F32),
        scratch_shapes=[pltpu.VMEM((tm, d), BF16)],
        compiler_params=_cparams(("parallel", "arbitrary")),
        name="norm_in_proj",
    )(x, g, w)


def _lru_kernel(x_ref, gate_ref, cw_ref, cb_ref, wa_ref, ba_ref, wx_ref, bx_ref, lam_ref,
                ng_ref, o_ref, xbuf, a_s, b_s, hcar):
    c = pl.program_id(1)
    ts = x_ref.shape[1]
    hist = SUBLANES

    @pl.when(c == 0)
    def _():
        xbuf[...] = jnp.zeros((hist, D_RNN), F32)
        hcar[...] = jnp.zeros((1, D_RNN), F32)

    x = x_ref[0]
    prev = xbuf[...]
    rows8 = lax.broadcasted_iota(jnp.int32, (SUBLANES, D_RNN), 0)
    xc = cb_ref[...]
    for tap in range(CONV_WIDTH):
        back = CONV_WIDTH - 1 - tap
        if back == 0:
            xs = x
        else:
            rolled = pltpu.roll(x, back, axis=0)
            head = jnp.where(rows8 < back, pltpu.roll(prev, back, axis=0), rolled[0:hist, :])
            xs = jnp.concatenate([head, rolled[hist:, :]], axis=0)
        xc = xc + xs * cw_ref[tap:tap + 1, :]
    xbuf[...] = x[ts - hist:ts, :]

    xcb = xc.astype(BF16)
    ra = []
    rx = []
    for n in range(LRU_BLOCKS):
        xs = xcb[:, n * LRU_BLOCK:(n + 1) * LRU_BLOCK]
        ra.append(jnp.dot(xs, wa_ref[n], preferred_element_type=F32))
        rx.append(jnp.dot(xs, wx_ref[n], preferred_element_type=F32))
    r = jax.nn.sigmoid(jnp.concatenate(ra, axis=1) + ba_ref[...])
    ig = jax.nn.sigmoid(jnp.concatenate(rx, axis=1) + bx_ref[...])

    z = -lam_ref[...]
    softplus = jnp.maximum(z, 0.0) + jnp.log1p(jnp.exp(-jnp.abs(z)))
    a = jnp.exp2(r * ((-LRU_C * LOG2_E) * softplus))
    om = 1.0 - a * a
    mult = jnp.where(om > 0.0, om * lax.rsqrt(om), 0.0)
    a_s[...] = a
    b_s[...] = mult * ig * xc

    @pl.when(c == 0)
    def _():
        b_s[0:1, :] = ig[0:1, :] * xc[0:1, :]

    def body(i, h):
        r0 = pl.multiple_of(i * SUBLANES, SUBLANES)
        av = a_s[pl.ds(r0, SUBLANES), :]
        bv = b_s[pl.ds(r0, SUBLANES), :]
        for sh in (1, 2, 4):
            a_sh = pltpu.roll(av, sh, axis=0)
            b_sh = pltpu.roll(bv, sh, axis=0)
            m = rows8 >= sh
            bv = jnp.where(m, av * b_sh + bv, bv)
            av = jnp.where(m, av * a_sh, av)
        hv = av * h + bv
        b_s[pl.ds(r0, SUBLANES), :] = hv
        return hv[SUBLANES - 1:SUBLANES, :]

    hcar[...] = lax.fori_loop(0, ts // SUBLANES, body, hcar[...])

    y = _rms(b_s[...], ng_ref[...])
    o_ref[0] = (y * jax.nn.gelu(gate_ref[0])).astype(BF16)


def _lru_branch(proj3, cw, cb, wa, ba, wx, bx, lam, ng, casts):
    b, s, _ = proj3.shape
    ts = min(SEQ_TS, s)
    vec = pl.BlockSpec((1, D_RNN), lambda i, j: (0, 0))
    blk = pl.BlockSpec((LRU_BLOCKS, LRU_BLOCK, LRU_BLOCK), lambda i, j: (0, 0, 0))
    in_specs = [
        pl.BlockSpec((1, ts, D_RNN), lambda i, j: (i, j, 0)),
        pl.BlockSpec((1, ts, D_RNN), lambda i, j: (i, j, 1)),
        pl.BlockSpec((CONV_WIDTH, D_RNN), lambda i, j: (0, 0)),
        vec, blk, vec, blk, vec, vec, vec,
    ]
    cast_specs, cast_shapes = _cast_specs(casts, b * (s // ts), s // ts)
    outs = pl.pallas_call(
        _with_casts(_lru_kernel, len(in_specs), 1, len(casts)),
        grid=(b, s // ts),
        in_specs=in_specs + cast_specs,
        out_specs=[pl.BlockSpec((1, ts, D_RNN), lambda i, j: (i, j, 0))] + cast_specs,
        out_shape=[jax.ShapeDtypeStruct((b, s, D_RNN), BF16)] + cast_shapes,
        scratch_shapes=[
            pltpu.VMEM((SUBLANES, D_RNN), F32),
            pltpu.VMEM((ts, D_RNN), F32),
            pltpu.VMEM((ts, D_RNN), F32),
            pltpu.VMEM((1, D_RNN), F32),
        ],
        compiler_params=_cparams(("arbitrary", "arbitrary")),
        name="rglru_branch",
    )(proj3, proj3, cw, cb, wa, ba, wx, bx, lam, ng, *casts)
    return outs[0], outs[1:]


def _rope(x, cos, sin_signed):
    half = RET_DK // 2
    lane = lax.broadcasted_iota(jnp.int32, x.shape, 1)
    first = (lane & (RET_DK - 1)) < half
    rot = jnp.where(first, pltpu.roll(x, LANES - half, axis=1), pltpu.roll(x, half, axis=1))
    return x * cos + rot * sin_signed


def _ret_kernel(q_ref, k_ref, v_ref, g_ref, cos_ref, sin_ref, xim_ref, zmt_ref, dec_ref,
                cd_ref, gn_ref, o_ref, state):
    c = pl.program_id(1)
    ts = q_ref.shape[1]

    @pl.when(c == 0)
    def _():
        state[...] = jnp.zeros(state.shape, F32)

    lane = lax.broadcasted_iota(jnp.int32, (RET_CHUNK, LANES), 1)
    even_head = lane < RET_DK
    for p in range(RET_HEADS // 2):
        ps = slice(p * LANES, (p + 1) * LANES)
        for n in range(ts // RET_CHUNK):
            rows = slice(n * RET_CHUNK, (n + 1) * RET_CHUNK)
            cos = cos_ref[rows, :]
            sin = sin_ref[rows, :]
            qp = _rope(q_ref[0, rows, ps], cos, sin)
            kt = _rope(k_ref[0, rows, ps], cos, sin).T
            ktb = kt.astype(BF16)
            qm = jnp.concatenate([jnp.where(even_head, qp, 0.0), jnp.where(even_head, 0.0, qp)],
                                 axis=0).astype(BF16)
            sc = jnp.dot(qm, ktb, preferred_element_type=F32)
            for par in range(2):
                h = 2 * p + par
                vs = slice(h * RET_DV, (h + 1) * RET_DV)
                vh = v_ref[0, rows, vs].astype(BF16)
                scores = sc[par * RET_CHUNK:(par + 1) * RET_CHUNK, :] * dec_ref[h]
                inner = jnp.dot(scores.astype(BF16), vh, preferred_element_type=F32)
                st = state[h]
                cross = jnp.dot((qp * xim_ref[h]).astype(BF16), st.astype(BF16),
                                preferred_element_type=F32)
                kv = jnp.dot((kt * zmt_ref[h]).astype(BF16), vh, preferred_element_type=F32)
                state[h] = cd_ref[h] * st + kv
                o = inner + cross
                mu = jnp.mean(o, axis=-1, keepdims=True)
                d = o - mu
                var = jnp.mean(d * d, axis=-1, keepdims=True)
                y = d * lax.rsqrt(var + EPS) * gn_ref[:, vs]
                o_ref[0, rows, vs] = (y * jax.nn.silu(g_ref[0, rows, vs])).astype(BF16)


def _ret_tables(s):
    scale = RET_DK ** -0.5
    inv_freq = ROPE_BASE ** (-jnp.arange(0, RET_DK, 2, dtype=F32) / RET_DK)
    ang = jnp.arange(s, dtype=F32)[:, None] * inv_freq[None, :]
    cos = jnp.cos(ang)
    sin = jnp.sin(ang)
    cos_t = jnp.tile(cos, (1, 2 * LANES // RET_DK))
    sin_t = jnp.tile(jnp.concatenate([-sin, sin], axis=1), (1, LANES // RET_DK))
    log_g = jnp.log(1.0 - 2.0 ** (-5.0 - jnp.arange(RET_HEADS, dtype=F32)))
    idx = jnp.arange(RET_CHUNK, dtype=F32)
    rel = idx[:, None] - idx[None, :]
    decay = jnp.where(rel >= 0, jnp.exp(jnp.maximum(rel, 0.0)[None] * log_g[:, None, None]), 0.0)
    zeta = jnp.exp((RET_CHUNK - 1.0 - idx)[None, :] * log_g[:, None])
    xi = jnp.exp((idx + 1.0)[None, :] * log_g[:, None])
    own = (jnp.arange(LANES)[None, :] // RET_DK) == (jnp.arange(RET_HEADS)[:, None] % 2)
    xim = jnp.where(own[:, None, :], xi[:, :, None], 0.0)
    zmt = jnp.where(own[:, :, None], scale * zeta[:, None, :], 0.0)
    cdec = jnp.exp(RET_CHUNK * log_g)
    return cos_t, sin_t, xim, zmt, decay * scale, cdec


def _ret_branch(proj3, tables, gn, casts):
    b, s, _ = proj3.shape
    ts = min(SEQ_TS, s)
    cos_t, sin_t, xim, zmt, decay, cdec = tables
    q_blk = 2 * D_RNN // D_QK
    cast_specs, cast_shapes = _cast_specs(casts, b * (s // ts), s // ts)
    per_head = pl.BlockSpec((RET_HEADS, RET_CHUNK, RET_CHUNK), lambda i, j: (0, 0, 0))
    assert RET_CHUNK == LANES == RET_DV == 2 * RET_DK
    outs = pl.pallas_call(
        _with_casts(_ret_kernel, 11, 1, len(casts)),
        grid=(b, s // ts),
        in_specs=[
            pl.BlockSpec((1, ts, D_QK), lambda i, j: (i, j, q_blk)),
            pl.BlockSpec((1, ts, D_QK), lambda i, j: (i, j, q_blk + 1)),
            pl.BlockSpec((1, ts, D_RET), lambda i, j: (i, j, 3)),
            pl.BlockSpec((1, ts, D_RET), lambda i, j: (i, j, 4)),
            pl.BlockSpec((ts, LANES), lambda i, j: (j, 0)),
            pl.BlockSpec((ts, LANES), lambda i, j: (j, 0)),
            per_head, per_head, per_head,
            pl.BlockSpec(memory_space=pltpu.SMEM),
            pl.BlockSpec((1, D_RET), lambda i, j: (0, 0)),
        ] + cast_specs,
        out_specs=[pl.BlockSpec((1, ts, D_RET), lambda i, j: (i, j, 0))] + cast_specs,
        out_shape=[jax.ShapeDtypeStruct((b, s, D_RET), BF16)] + cast_shapes,
        scratch_shapes=[pltpu.VMEM((RET_HEADS, LANES, RET_DV), F32)],
        compiler_params=_cparams(("arbitrary", "arbitrary")),
        name="retention_branch",
    )(proj3, proj3, proj3, proj3, cos_t, sin_t, xim, zmt, decay, cdec, gn, *casts)
    return outs[0], outs[1:]


def _out_proj_kernel(yl_ref, yr_ref, x_ref, w_ref, g_ref, x1_ref, h_ref):
    acc = jnp.dot(yl_ref[...], w_ref[0:D_RNN, :], preferred_element_type=F32)
    acc = acc + jnp.dot(yr_ref[...], w_ref[D_RNN:D_MODEL, :], preferred_element_type=F32)
    x1 = x_ref[...] + acc
    x1_ref[...] = x1
    h_ref[...] = _rms(x1, g_ref[...]).astype(BF16)


def _out_proj_router_kernel(yl_ref, yr_ref, x_ref, w_ref, g_ref, rw_ref, x1_ref, idx_ref, gate_ref):
    acc = jnp.dot(yl_ref[...], w_ref[0:D_RNN, :], preferred_element_type=F32)
    acc = acc + jnp.dot(yr_ref[...], w_ref[D_RNN:D_MODEL, :], preferred_element_type=F32)
    x1 = x_ref[...] + acc
    x1_ref[...] = x1
    h = _rms(x1, g_ref[...]).astype(BF16)
    logits = jnp.dot(h, rw_ref[...], preferred_element_type=F32)
    lane_i = lax.broadcasted_iota(jnp.int32, logits.shape, 1)
    lane = lane_i.astype(F32)
    neg = jnp.float32(-jnp.inf)
    logits = jnp.where(lane_i < N_EXPERTS, logits, neg)
    m1 = jnp.max(logits, axis=-1, keepdims=True)
    i1 = jnp.min(jnp.where(logits == m1, lane, float(LANES)), axis=-1, keepdims=True)
    rest = jnp.where(lane == i1, neg, logits)
    m2 = jnp.max(rest, axis=-1, keepdims=True)
    i2 = jnp.min(jnp.where(rest == m2, lane, float(LANES)), axis=-1, keepdims=True)
    e2 = jnp.exp(m2 - m1)
    den = 1.0 + e2
    idx_ref[...] = jnp.where(lane_i == 0, i1, i2).astype(jnp.int32)
    gate_ref[...] = jnp.where(lane_i == 0, 1.0 / den, e2 / den)


def _out_proj(yl, yr, x, w, g, router=None):
    t, d = x.shape
    tm = min(OUT_TM, t)
    row = lambda i: (i, 0)
    fixed = lambda i: (0, 0)
    in_specs = [
        pl.BlockSpec((tm, D_RNN), row),
        pl.BlockSpec((tm, D_RET), row),
        pl.BlockSpec((tm, d), row),
        pl.BlockSpec((d, d), fixed),
        pl.BlockSpec((1, d), fixed),
    ]
    if router is None:
        return pl.pallas_call(
            _out_proj_kernel,
            grid=(t // tm,),
            in_specs=in_specs,
            out_specs=[pl.BlockSpec((tm, d), row), pl.BlockSpec((tm, d), row)],
            out_shape=[jax.ShapeDtypeStruct((t, d), F32), jax.ShapeDtypeStruct((t, d), BF16)],
            compiler_params=_cparams(("parallel",)),
            name="out_proj_norm",
        )(yl, yr, x, w, g)
    return pl.pallas_call(
        _out_proj_router_kernel,
        grid=(t // tm,),
        in_specs=in_specs + [pl.BlockSpec((d, LANES), fixed)],
        out_specs=[pl.BlockSpec((tm, d), row), pl.BlockSpec((tm, LANES), row),
                   pl.BlockSpec((tm, LANES), row)],
        out_shape=[jax.ShapeDtypeStruct((t, d), F32), jax.ShapeDtypeStruct((t, LANES), jnp.int32),
                   jax.ShapeDtypeStruct((t, LANES), F32)],
        compiler_params=_cparams(("parallel",)),
        name="out_proj_router",
    )(yl, yr, x, w, g, router)


def _swiglu_step(h, wg, wu, wd):
    gt = jnp.dot(h, wg, preferred_element_type=F32)
    up = jnp.dot(h, wu, preferred_element_type=F32)
    act = (jax.nn.silu(gt) * up).astype(BF16)
    return jnp.dot(act, wd, preferred_element_type=F32)


def _ffn_kernel(h_ref, x_ref, wg_ref, wu_ref, wd_ref, o_ref):
    @pl.when(pl.program_id(1) == 0)
    def _():
        o_ref[...] = x_ref[...]

    o_ref[...] += _swiglu_step(h_ref[...], wg_ref[...], wu_ref[...], wd_ref[...])


def _dense_ffn(h, x, wg, wu, wd):
    t, d = x.shape
    tm = min(FFN_TM, t)
    tf = FFN_TF
    return pl.pallas_call(
        _ffn_kernel,
        grid=(t // tm, D_FF // tf),
        in_specs=[
            pl.BlockSpec((tm, d), lambda i, f: (i, 0)),
            pl.BlockSpec((tm, d), lambda i, f: (i, 0)),
            pl.BlockSpec((d, tf), lambda i, f: (0, f)),
            pl.BlockSpec((d, tf), lambda i, f: (0, f)),
            pl.BlockSpec((tf, d), lambda i, f: (f, 0)),
        ],
        out_specs=pl.BlockSpec((tm, d), lambda i, f: (i, 0)),
        out_shape=jax.ShapeDtypeStruct((t, d), F32),
        compiler_params=_cparams(("parallel", "arbitrary")),
        name="dense_swiglu",
    )(h, x, wg, wu, wd)


def _moe_kernel(te_ref, tv_ref, src_ref, nxt_ref, x_hbm, g_ref, wg_ref, wu_ref, wd_ref, o_ref,
                xg, hs, sem):
    i = pl.program_id(0)
    f = pl.program_id(1)
    groups = xg.shape[1]
    tm = groups * SUBLANES
    slot = i % 2

    def start_rows(idx_ref, dst_slot):
        def body(grp, carry):
            for sub in range(SUBLANES):
                tok = idx_ref[0, 0, grp * SUBLANES + sub]
                pltpu.make_async_copy(x_hbm.at[pl.ds(tok, 1)],
                                      xg.at[dst_slot, grp, pl.ds(sub, 1)],
                                      sem.at[dst_slot]).start()
            return carry

        lax.fori_loop(0, groups, body, 0)

    @pl.when((i == 0) & (f == 0))
    def _():
        start_rows(src_ref, 0)

    @pl.when(f == 0)
    def _():
        pltpu.make_async_copy(xg.at[1 - slot], xg.at[slot], sem.at[slot]).wait()

        @pl.when(i + 1 < pl.num_programs(0))
        def _():
            start_rows(nxt_ref, 1 - slot)

        rows = xg[slot].reshape(tm, xg.shape[3])
        hs[...] = _rms(rows, g_ref[...]).astype(BF16)
        o_ref[...] = jnp.zeros(o_ref.shape, F32)

    @pl.when(tv_ref[i] == 1)
    def _():
        o_ref[...] += _swiglu_step(hs[...], wg_ref[0], wu_ref[0], wd_ref[0])


def _moe_ffn(x, g, tile_expert, tile_valid, src3, wg, wu, wd):
    t, d = x.shape
    n_tiles, _, tm = src3.shape
    tf = FFN_TF
    nf = D_FF // tf

    def w_col(i, f, te, tv):
        return (te[i], 0, jnp.where(tv[i] == 1, f, nf - 1))

    def w_row(i, f, te, tv):
        return (te[i], jnp.where(tv[i] == 1, f, nf - 1), 0)

    grid_spec = pltpu.PrefetchScalarGridSpec(
        num_scalar_prefetch=2,
        grid=(n_tiles, nf),
        in_specs=[
            pl.BlockSpec((1, 1, tm), lambda i, f, te, tv: (i, 0, 0), memory_space=pltpu.SMEM),
            pl.BlockSpec((1, 1, tm), lambda i, f, te, tv: (jnp.minimum(i + 1, n_tiles - 1), 0, 0),
                         memory_space=pltpu.SMEM),
            pl.BlockSpec(memory_space=pl.ANY),
            pl.BlockSpec((1, d), lambda i, f, te, tv: (0, 0)),
            pl.BlockSpec((1, d, tf), w_col),
            pl.BlockSpec((1, d, tf), w_col),
            pl.BlockSpec((1, tf, d), w_row),
        ],
        out_specs=pl.BlockSpec((tm, d), lambda i, f, te, tv: (i, 0)),
        scratch_shapes=[
            pltpu.VMEM((2, tm // SUBLANES, SUBLANES, d), F32),
            pltpu.VMEM((tm, d), BF16),
            pltpu.SemaphoreType.DMA((2,)),
        ],
    )
    return pl.pallas_call(
        _moe_kernel,
        grid_spec=grid_spec,
        out_shape=jax.ShapeDtypeStruct((n_tiles * tm, d), F32),
        compiler_params=_cparams(("arbitrary", "arbitrary")),
        name="expert_swiglu",
    )(tile_expert, tile_valid, src3, src3, x, g, wg, wu, wd)


def _combine_kernel(pos_ref, nxt_ref, x_ref, gate_ref, g_ref, ys_hbm, o_ref, buf, sem):
    i = pl.program_id(0)
    tt = x_ref.shape[0]
    slot = i % 2

    def start_rows(idx_ref, dst_slot):
        def body(grp, carry):
            for sub in range(SUBLANES):
                for k in range(TOP_K):
                    p = idx_ref[0, 0, TOP_K * (grp * SUBLANES + sub) + k]
                    pltpu.make_async_copy(ys_hbm.at[pl.ds(p, 1)],
                                          buf.at[dst_slot, k, grp, pl.ds(sub, 1)],
                                          sem.at[dst_slot]).start()
            return carry

        lax.fori_loop(0, tt // SUBLANES, body, 0)

    @pl.when(i == 0)
    def _():
        start_rows(pos_ref, 0)

    pltpu.make_async_copy(buf.at[1 - slot], buf.at[slot], sem.at[slot]).wait()

    @pl.when(i + 1 < pl.num_programs(0))
    def _():
        start_rows(nxt_ref, 1 - slot)

    d = x_ref.shape[1]
    gates = gate_ref[...]
    moe = gates[:, 0:1] * buf[slot, 0].reshape(tt, d)
    for k in range(1, TOP_K):
        moe = moe + gates[:, k:k + 1] * buf[slot, k].reshape(tt, d)
    o_ref[...] = _rms(x_ref[...] + moe, g_ref[...])


def _combine(x, gates, pos3, ys, g):
    t, d = x.shape
    n_steps, _, per = pos3.shape
    tt = per // TOP_K
    return pl.pallas_call(
        _combine_kernel,
        grid=(n_steps,),
        in_specs=[
            pl.BlockSpec((1, 1, per), lambda i: (i, 0, 0), memory_space=pltpu.SMEM),
            pl.BlockSpec((1, 1, per), lambda i: (jnp.minimum(i + 1, n_steps - 1), 0, 0),
                         memory_space=pltpu.SMEM),
            pl.BlockSpec((tt, d), lambda i: (i, 0)),
            pl.BlockSpec((tt, LANES), lambda i: (i, 0)),
            pl.BlockSpec((1, d), lambda i: (0, 0)),
            pl.BlockSpec(memory_space=pl.ANY),
        ],
        out_specs=pl.BlockSpec((tt, d), lambda i: (i, 0)),
        out_shape=jax.ShapeDtypeStruct((t, d), F32),
        scratch_shapes=[pltpu.VMEM((2, TOP_K, tt // SUBLANES, SUBLANES, d), F32),
                        pltpu.SemaphoreType.DMA((2,))],
        compiler_params=_cparams(("arbitrary",)),
        name="combine_final_norm",
    )(pos3, pos3, x, gates, g, ys)


def _routing_plan(top_idx, tm):
    t = top_idx.shape[0]
    n_slots = t * TOP_K
    e_flat = top_idx.reshape(n_slots)
    onehot = (e_flat[:, None] == jnp.arange(N_EXPERTS, dtype=jnp.int32)[None, :]).astype(jnp.int32)
    csum = jnp.cumsum(onehot, axis=0)
    counts = csum[-1]
    rank = jnp.sum((csum - onehot) * onehot, axis=1)
    padded = ((counts + tm - 1) // tm) * tm
    ends = jnp.cumsum(padded)
    starts = ends - padded
    pos = jnp.sum(starts[None, :] * onehot, axis=1) + rank
    n_tiles = n_slots // tm + N_EXPERTS
    src = jnp.zeros((n_tiles * tm,), jnp.int32).at[pos].set(
        jnp.arange(n_slots, dtype=jnp.int32) // TOP_K)
    tile_start = jnp.arange(n_tiles, dtype=jnp.int32) * tm
    tile_expert = jnp.minimum(
        jnp.sum((tile_start[:, None] >= ends[None, :]).astype(jnp.int32), axis=1), N_EXPERTS - 1)
    tile_valid = (tile_start < ends[-1]).astype(jnp.int32)
    return pos, src.reshape(n_tiles, 1, tm), tile_expert, tile_valid


def kernel(x, norm1_g, w_in, conv_w, conv_b, lru_wa, lru_ba, lru_wx, lru_bx, lru_lambda,
           lru_norm_g, ret_norm_g, w_out, norm2_g, ffn_w_gate, ffn_w_up, ffn_w_down,
           moe_router, moe_w_gate, moe_w_up, moe_w_down, final_norm_g):
    assert DEPTH == 2 and w_in.shape[0] == DEPTH
    b, s, d = x.shape
    t = b * s
    tables = _ret_tables(s)
    xf = x.reshape(t, d)
    row = lambda v: v.reshape(1, -1)

    def branches(layer, x_in, w_in_bf16, lru_casts, ret_casts):
        proj = _norm_matmul(x_in, row(norm1_g[layer]), w_in_bf16)
        proj3 = proj.reshape(b, s, D_IN)
        y_lru, lru_cast = _lru_branch(proj3, conv_w[layer], row(conv_b[layer]),
                                      lru_wa[layer].astype(BF16), row(lru_ba[layer]),
                                      lru_wx[layer].astype(BF16), row(lru_bx[layer]),
                                      row(lru_lambda[layer]), row(lru_norm_g[layer]), lru_casts)
        y_ret, ret_cast = _ret_branch(proj3, tables, row(ret_norm_g[layer]), ret_casts)
        return y_lru.reshape(t, D_RNN), y_ret.reshape(t, D_RET), lru_cast, ret_cast

    flat2 = lambda w: w.reshape(-1, w.shape[-1])
    yl, yr, (wg0, wu0, wd0, wo0, wi1, wo1), (moe_g,) = branches(
        0, xf, w_in[0].astype(BF16),
        [ffn_w_gate[0], ffn_w_up[0], ffn_w_down[0], w_out[0], w_in[1], w_out[1]],
        [flat2(moe_w_gate[0])])
    x1, h2 = _out_proj(yl, yr, xf, wo0, row(norm2_g[0]))
    xf = _dense_ffn(h2, x1, wg0, wu0, wd0)
    yl, yr, (moe_u,), (moe_d,) = branches(1, xf, wi1, [flat2(moe_w_up[0])],
                                          [flat2(moe_w_down[0])])
    router = jnp.pad(moe_router[0], ((0, 0), (0, LANES - N_EXPERTS))).astype(BF16)
    x1, idx, gates = _out_proj(yl, yr, xf, wo1, row(norm2_g[1]), router)
    tm = min(FFN_TM, t)
    pos, src3, tile_expert, tile_valid = _routing_plan(idx[:, :TOP_K], tm)
    ys = _moe_ffn(x1, row(norm2_g[1]), tile_expert, tile_valid, src3,
                  moe_g.reshape(moe_w_gate.shape[1:]), moe_u.reshape(moe_w_up.shape[1:]),
                  moe_d.reshape(moe_w_down.shape[1:]))
    tt = min(COMB_TT, t)
    pos3 = pos.reshape(t // tt, 1, tt * TOP_K)
    out = _combine(x1, gates, pos3, ys, row(final_norm_g))
    return out.reshape(b, s, d)
```

```python
import functools

import jax
import jax.numpy as jnp
from jax import lax
from jax.experimental import pallas as pl
from jax.experimental.pallas import tpu as pltpu

F32 = jnp.float32
BF16 = jnp.bfloat16

D_MODEL = 2048
DEPTH = 2
D_RNN = D_MODEL // 2
D_RET = D_MODEL - D_RNN
LRU_BLOCKS = 8
LRU_BLOCK = D_RNN // LRU_BLOCKS
CONV_WIDTH = 4
LRU_C = 8.0
RET_HEADS = 8
RET_DV = D_RET // RET_HEADS
RET_DK = RET_DV // 2
RET_CHUNK = 128
ROPE_BASE = 10000.0
D_FF = 256 * ((8 * D_MODEL // 3 + 255) // 256)
N_EXPERTS = 8
TOP_K = 2
EPS = 1e-6
LOG2_E = 1.4426950408889634
D_QK = RET_HEADS * RET_DK
D_IN = 2 * D_RNN + 2 * D_QK + 2 * D_RET

LANES = 128
SUBLANES = 8
VMEM_LIMIT_BYTES = 56 * 1024 * 1024

PROJ_TM = 1024
PROJ_TN = 1024
SEQ_TS = 256
OUT_TM = 512
FFN_TM = 512
FFN_TF = 512
COMB_TT = 256


def _cparams(sem):
    return pltpu.CompilerParams(dimension_semantics=sem, vmem_limit_bytes=VMEM_LIMIT_BYTES)


def _rms(x, g):
    return x * lax.rsqrt(jnp.mean(x * x, axis=-1, keepdims=True) + EPS) * g


def _with_casts(body, n_in, n_out, n_cast):
    def kern(*refs):
        ins = refs[:n_in]
        cast_in = refs[n_in:n_in + n_cast]
        outs = refs[n_in + n_cast:n_in + n_cast + n_out]
        cast_out = refs[n_in + n_cast + n_out:n_in + 2 * n_cast + n_out]
        scratch = refs[n_in + 2 * n_cast + n_out:]
        for src, dst in zip(cast_in, cast_out):
            _cast_block(src, dst)
        body(*ins, *outs, *scratch)

    return kern


def _cast_block(src, dst):
    if len(dst.shape) == 2:
        dst[...] = src[...].astype(BF16)
        return
    n, _, tf = dst.shape[-3:]
    lead = (0,) * (len(dst.shape) - 3)
    for f in range(n):
        dst[lead + (f,)] = src[:, f * tf:(f + 1) * tf].astype(BF16)


class _Cast:
    def __init__(self, array, in_spec, out_spec, out_shape):
        self.array, self.in_spec, self.out_spec = array, in_spec, out_spec
        self.out_shape = jax.ShapeDtypeStruct(out_shape, BF16)


def _flat_cast(a, n_steps, step_of):
    rows, cols = a.shape
    spec = pl.BlockSpec((rows // n_steps, cols), lambda *ids: (step_of(*ids), 0))
    return _Cast(a, spec, spec, (rows, cols))


def _tiled_cast(a, n_experts, n_steps, step_of):
    rows_all, cols = a.shape
    rows = rows_all // n_experts
    blk = rows_all // n_steps
    per = rows // blk
    in_spec = pl.BlockSpec((blk, cols), lambda *ids: (step_of(*ids), 0))
    out_spec = pl.BlockSpec((1, cols // FFN_TF, blk, FFN_TF),
                            lambda *ids: (step_of(*ids) // per, 0, step_of(*ids) % per, 0))
    return _Cast(a, in_spec, out_spec, (n_experts, cols // FFN_TF, rows, FFN_TF))


def _grid_tiled_cast(a, n_experts, n_row_steps):
    rows_all, cols = a.shape
    rows = rows_all // n_experts
    blk = rows_all // n_row_steps
    per = rows // blk
    in_spec = pl.BlockSpec((blk, FFN_TF), lambda i, f: (i, f))
    out_spec = pl.BlockSpec((1, 1, blk, FFN_TF), lambda i, f: (i // per, f, i % per, 0))
    return _Cast(a, in_spec, out_spec, (n_experts, cols // FFN_TF, rows, FFN_TF))


def _norm_matmul_kernel(x_ref, g_ref, w_ref, o_ref, h_ref):
    @pl.when(pl.program_id(1) == 0)
    def _():
        h_ref[...] = _rms(x_ref[...], g_ref[...]).astype(BF16)

    o_ref[...] = jnp.dot(h_ref[...], w_ref[...], preferred_element_type=F32)


def _norm_matmul(x, g, w):
    t, d = x.shape
    n = w.shape[1]
    tm = min(PROJ_TM, t)
    tn = PROJ_TN
    return pl.pallas_call(
        _norm_matmul_kernel,
        grid=(t // tm, n // tn),
        in_specs=[
            pl.BlockSpec((tm, d), lambda i, j: (i, 0)),
            pl.BlockSpec((1, d), lambda i, j: (0, 0)),
            pl.BlockSpec((d, tn), lambda i, j: (0, j)),
        ],
        out_specs=pl.BlockSpec((tm, tn), lambda i, j: (i, j)),
        out_shape=jax.ShapeDtypeStruct((t, n), F32),
        scratch_shapes=[pltpu.VMEM((tm, d), BF16)],
        compiler_params=_cparams(("parallel", "arbitrary")),
        name="norm_in_proj",
    )(x, g, w)


def _lru_kernel(x_ref, gate_ref, cw_ref, cb_ref, wa_ref, ba_ref, wx_ref, bx_ref, lam_ref,
                ng_ref, o_ref, xbuf, a_s, b_s, hcar):
    c = pl.program_id(1)
    ts = x_ref.shape[1]
    hist = SUBLANES

    @pl.when(c == 0)
    def _():
        xbuf[...] = jnp.zeros((hist, D_RNN), F32)
        hcar[...] = jnp.zeros((1, D_RNN), F32)

    x = x_ref[0]
    prev = xbuf[...]
    rows8 = lax.broadcasted_iota(jnp.int32, (SUBLANES, D_RNN), 0)
    xc = cb_ref[...]
    for tap in range(CONV_WIDTH):
        back = CONV_WIDTH - 1 - tap
        if back == 0:
            xs = x
        else:
            rolled = pltpu.roll(x, back, axis=0)
            head = jnp.where(rows8 < back, pltpu.roll(prev, back, axis=0), rolled[0:hist, :])
            xs = jnp.concatenate([head, rolled[hist:, :]], axis=0)
        xc = xc + xs * cw_ref[tap:tap + 1, :]
    xbuf[...] = x[ts - hist:ts, :]

    xcb = xc.astype(BF16)
    ra = []
    rx = []
    for n in range(LRU_BLOCKS):
        xs = xcb[:, n * LRU_BLOCK:(n + 1) * LRU_BLOCK]
        ra.append(jnp.dot(xs, wa_ref[n], preferred_element_type=F32))
        rx.append(jnp.dot(xs, wx_ref[n], preferred_element_type=F32))
    r = jax.nn.sigmoid(jnp.concatenate(ra, axis=1) + ba_ref[...])
    ig = jax.nn.sigmoid(jnp.concatenate(rx, axis=1) + bx_ref[...])

    z = -lam_ref[...]
    softplus = jnp.maximum(z, 0.0) + jnp.log1p(jnp.exp(-jnp.abs(z)))
    a = jnp.exp2(r * ((-LRU_C * LOG2_E) * softplus))
    om = 1.0 - a * a
    mult = jnp.where(om > 0.0, om * lax.rsqrt(om), 0.0)
    a_s[...] = a
    b_s[...] = mult * ig * xc

    @pl.when(c == 0)
    def _():
        b_s[0:1, :] = ig[0:1, :] * xc[0:1, :]

    def body(i, h):
        r0 = pl.multiple_of(i * SUBLANES, SUBLANES)
        av = a_s[pl.ds(r0, SUBLANES), :]
        bv = b_s[pl.ds(r0, SUBLANES), :]
        for sh in (1, 2, 4):
            a_sh = pltpu.roll(av, sh, axis=0)
            b_sh = pltpu.roll(bv, sh, axis=0)
            m = rows8 >= sh
            bv = jnp.where(m, av * b_sh + bv, bv)
            av = jnp.where(m, av * a_sh, av)
        hv = av * h + bv
        b_s[pl.ds(r0, SUBLANES), :] = hv
        return hv[SUBLANES - 1:SUBLANES, :]

    hcar[...] = lax.fori_loop(0, ts // SUBLANES, body, hcar[...])

    y = _rms(b_s[...], ng_ref[...])
    o_ref[0] = (y * jax.nn.gelu(gate_ref[0])).astype(BF16)


def _lru_branch(proj3, cw, cb, wa, ba, wx, bx, lam, ng, casts):
    b, s, _ = proj3.shape
    ts = min(SEQ_TS, s)
    vec = pl.BlockSpec((1, D_RNN), lambda i, j: (0, 0))
    blk = pl.BlockSpec((LRU_BLOCKS, LRU_BLOCK, LRU_BLOCK), lambda i, j: (0, 0, 0))
    in_specs = [
        pl.BlockSpec((1, ts, D_RNN), lambda i, j: (i, j, 0)),
        pl.BlockSpec((1, ts, D_RNN), lambda i, j: (i, j, 1)),
        pl.BlockSpec((CONV_WIDTH, D_RNN), lambda i, j: (0, 0)),
        vec, blk, vec, blk, vec, vec, vec,
    ]
    outs = pl.pallas_call(
        _with_casts(_lru_kernel, len(in_specs), 1, len(casts)),
        grid=(b, s // ts),
        in_specs=in_specs + [c.in_spec for c in casts],
        out_specs=[pl.BlockSpec((1, ts, D_RNN), lambda i, j: (i, j, 0))]
        + [c.out_spec for c in casts],
        out_shape=[jax.ShapeDtypeStruct((b, s, D_RNN), BF16)] + [c.out_shape for c in casts],
        scratch_shapes=[
            pltpu.VMEM((SUBLANES, D_RNN), F32),
            pltpu.VMEM((ts, D_RNN), F32),
            pltpu.VMEM((ts, D_RNN), F32),
            pltpu.VMEM((1, D_RNN), F32),
        ],
        compiler_params=_cparams(("arbitrary", "arbitrary")),
        name="rglru_branch",
    )(proj3, proj3, cw, cb, wa, ba, wx, bx, lam, ng, *[c.array for c in casts])
    return outs[0], outs[1:]


def _rope(x, cos, sin_signed):
    half = RET_DK // 2
    lane = lax.broadcasted_iota(jnp.int32, x.shape, 1)
    first = (lane & (RET_DK - 1)) < half
    rot = jnp.where(first, pltpu.roll(x, LANES - half, axis=1), pltpu.roll(x, half, axis=1))
    return x * cos + rot * sin_signed


def _ret_kernel(q_ref, k_ref, v_ref, g_ref, cos_ref, sin_ref, xim_ref, zmt_ref, dec_ref,
                cd_ref, gn_ref, o_ref, state):
    c = pl.program_id(1)
    ts = q_ref.shape[1]

    @pl.when(c == 0)
    def _():
        state[...] = jnp.zeros(state.shape, F32)

    lane = lax.broadcasted_iota(jnp.int32, (RET_CHUNK, LANES), 1)
    even_head = lane < RET_DK
    for p in range(RET_HEADS // 2):
        ps = slice(p * LANES, (p + 1) * LANES)
        for n in range(ts // RET_CHUNK):
            rows = slice(n * RET_CHUNK, (n + 1) * RET_CHUNK)
            cos = cos_ref[rows, :]
            sin = sin_ref[rows, :]
            qp = _rope(q_ref[0, rows, ps], cos, sin)
            kt = _rope(k_ref[0, rows, ps], cos, sin).T
            ktb = kt.astype(BF16)
            qm = jnp.concatenate([jnp.where(even_head, qp, 0.0), jnp.where(even_head, 0.0, qp)],
                                 axis=0).astype(BF16)
            sc = jnp.dot(qm, ktb, preferred_element_type=F32)
            for par in range(2):
                h = 2 * p + par
                vs = slice(h * RET_DV, (h + 1) * RET_DV)
                vh = v_ref[0, rows, vs].astype(BF16)
                scores = sc[par * RET_CHUNK:(par + 1) * RET_CHUNK, :] * dec_ref[h]
                inner = jnp.dot(scores.astype(BF16), vh, preferred_element_type=F32)
                st = state[h]
                cross = jnp.dot((qp * xim_ref[h]).astype(BF16), st.astype(BF16),
                                preferred_element_type=F32)
                kv = jnp.dot((kt * zmt_ref[h]).astype(BF16), vh, preferred_element_type=F32)
                state[h] = cd_ref[h] * st + kv
                o = inner + cross
                mu = jnp.mean(o, axis=-1, keepdims=True)
                d = o - mu
                var = jnp.mean(d * d, axis=-1, keepdims=True)
                y = d * lax.rsqrt(var + EPS) * gn_ref[:, vs]
                o_ref[0, rows, vs] = (y * jax.nn.silu(g_ref[0, rows, vs])).astype(BF16)


def _ret_tables(s):
    scale = RET_DK ** -0.5
    inv_freq = ROPE_BASE ** (-jnp.arange(0, RET_DK, 2, dtype=F32) / RET_DK)
    ang = jnp.arange(s, dtype=F32)[:, None] * inv_freq[None, :]
    cos = jnp.cos(ang)
    sin = jnp.sin(ang)
    cos_t = jnp.tile(cos, (1, 2 * LANES // RET_DK))
    sin_t = jnp.tile(jnp.concatenate([-sin, sin], axis=1), (1, LANES // RET_DK))
    log_g = jnp.log(1.0 - 2.0 ** (-5.0 - jnp.arange(RET_HEADS, dtype=F32)))
    idx = jnp.arange(RET_CHUNK, dtype=F32)
    rel = idx[:, None] - idx[None, :]
    decay = jnp.where(rel >= 0, jnp.exp(jnp.maximum(rel, 0.0)[None] * log_g[:, None, None]), 0.0)
    zeta = jnp.exp((RET_CHUNK - 1.0 - idx)[None, :] * log_g[:, None])
    xi = jnp.exp((idx + 1.0)[None, :] * log_g[:, None])
    own = (jnp.arange(LANES)[None, :] // RET_DK) == (jnp.arange(RET_HEADS)[:, None] % 2)
    xim = jnp.where(own[:, None, :], xi[:, :, None], 0.0)
    zmt = jnp.where(own[:, :, None], scale * zeta[:, None, :], 0.0)
    cdec = jnp.exp(RET_CHUNK * log_g)
    return cos_t, sin_t, xim, zmt, decay * scale, cdec


def _ret_branch(proj3, tables, gn):
    b, s, _ = proj3.shape
    ts = min(SEQ_TS, s)
    cos_t, sin_t, xim, zmt, decay, cdec = tables
    q_blk = 2 * D_RNN // D_QK
    per_head = pl.BlockSpec((RET_HEADS, RET_CHUNK, RET_CHUNK), lambda i, j: (0, 0, 0))
    assert RET_CHUNK == LANES == RET_DV == 2 * RET_DK
    return pl.pallas_call(
        _ret_kernel,
        grid=(b, s // ts),
        in_specs=[
            pl.BlockSpec((1, ts, D_QK), lambda i, j: (i, j, q_blk)),
            pl.BlockSpec((1, ts, D_QK), lambda i, j: (i, j, q_blk + 1)),
            pl.BlockSpec((1, ts, D_RET), lambda i, j: (i, j, 3)),
            pl.BlockSpec((1, ts, D_RET), lambda i, j: (i, j, 4)),
            pl.BlockSpec((ts, LANES), lambda i, j: (j, 0)),
            pl.BlockSpec((ts, LANES), lambda i, j: (j, 0)),
            per_head, per_head, per_head,
            pl.BlockSpec(memory_space=pltpu.SMEM),
            pl.BlockSpec((1, D_RET), lambda i, j: (0, 0)),
        ],
        out_specs=pl.BlockSpec((1, ts, D_RET), lambda i, j: (i, j, 0)),
        out_shape=jax.ShapeDtypeStruct((b, s, D_RET), BF16),
        scratch_shapes=[pltpu.VMEM((RET_HEADS, LANES, RET_DV), F32)],
        compiler_params=_cparams(("parallel", "arbitrary")),
        name="retention_branch",
    )(proj3, proj3, proj3, proj3, cos_t, sin_t, xim, zmt, decay, cdec, gn)


def _out_proj_kernel(yl_ref, yr_ref, x_ref, w_ref, g_ref, x1_ref, h_ref):
    acc = jnp.dot(yl_ref[...], w_ref[0:D_RNN, :], preferred_element_type=F32)
    acc = acc + jnp.dot(yr_ref[...], w_ref[D_RNN:D_MODEL, :], preferred_element_type=F32)
    x1 = x_ref[...] + acc
    x1_ref[...] = x1
    h_ref[...] = _rms(x1, g_ref[...]).astype(BF16)


def _out_proj_router_kernel(yl_ref, yr_ref, x_ref, w_ref, g_ref, rw_ref, x1_ref, idx_ref, gate_ref):
    acc = jnp.dot(yl_ref[...], w_ref[0:D_RNN, :], preferred_element_type=F32)
    acc = acc + jnp.dot(yr_ref[...], w_ref[D_RNN:D_MODEL, :], preferred_element_type=F32)
    x1 = x_ref[...] + acc
    x1_ref[...] = x1
    h = _rms(x1, g_ref[...]).astype(BF16)
    logits = jnp.dot(h, rw_ref[...], preferred_element_type=F32)
    lane_i = lax.broadcasted_iota(jnp.int32, logits.shape, 1)
    lane = lane_i.astype(F32)
    neg = jnp.float32(-jnp.inf)
    logits = jnp.where(lane_i < N_EXPERTS, logits, neg)
    m1 = jnp.max(logits, axis=-1, keepdims=True)
    i1 = jnp.min(jnp.where(logits == m1, lane, float(LANES)), axis=-1, keepdims=True)
    rest = jnp.where(lane == i1, neg, logits)
    m2 = jnp.max(rest, axis=-1, keepdims=True)
    i2 = jnp.min(jnp.where(rest == m2, lane, float(LANES)), axis=-1, keepdims=True)
    e2 = jnp.exp(m2 - m1)
    den = 1.0 + e2
    idx_ref[...] = jnp.where(lane_i == 0, i1, i2).astype(jnp.int32)
    gate_ref[...] = jnp.where(lane_i == 0, 1.0 / den, e2 / den)


def _out_proj(yl, yr, x, w, g, router=None):
    t, d = x.shape
    tm = min(OUT_TM, t)
    row = lambda i: (i, 0)
    fixed = lambda i: (0, 0)
    in_specs = [
        pl.BlockSpec((tm, D_RNN), row),
        pl.BlockSpec((tm, D_RET), row),
        pl.BlockSpec((tm, d), row),
        pl.BlockSpec((d, d), fixed),
        pl.BlockSpec((1, d), fixed),
    ]
    if router is None:
        return pl.pallas_call(
            _out_proj_kernel,
            grid=(t // tm,),
            in_specs=in_specs,
            out_specs=[pl.BlockSpec((tm, d), row), pl.BlockSpec((tm, d), row)],
            out_shape=[jax.ShapeDtypeStruct((t, d), F32), jax.ShapeDtypeStruct((t, d), BF16)],
            compiler_params=_cparams(("parallel",)),
            name="out_proj_norm",
        )(yl, yr, x, w, g)
    return pl.pallas_call(
        _out_proj_router_kernel,
        grid=(t // tm,),
        in_specs=in_specs + [pl.BlockSpec((d, LANES), fixed)],
        out_specs=[pl.BlockSpec((tm, d), row), pl.BlockSpec((tm, LANES), row),
                   pl.BlockSpec((tm, LANES), row)],
        out_shape=[jax.ShapeDtypeStruct((t, d), F32), jax.ShapeDtypeStruct((t, LANES), jnp.int32),
                   jax.ShapeDtypeStruct((t, LANES), F32)],
        compiler_params=_cparams(("parallel",)),
        name="out_proj_router",
    )(yl, yr, x, w, g, router)


def _swiglu_step(h, wg, wu, wd):
    gt = jnp.dot(h, wg, preferred_element_type=F32)
    up = jnp.dot(h, wu, preferred_element_type=F32)
    act = (jax.nn.silu(gt) * up).astype(BF16)
    return jnp.dot(act, wd, preferred_element_type=F32)


def _ffn_kernel(h_ref, x_ref, wg_ref, wu_ref, wd_ref, o_ref):
    @pl.when(pl.program_id(1) == 0)
    def _():
        o_ref[...] = x_ref[...]

    o_ref[...] += _swiglu_step(h_ref[...], wg_ref[0], wu_ref[0], wd_ref[...])


def _dense_ffn(h, x, wg, wu, wd, make_casts):
    t, d = x.shape
    tm = min(FFN_TM, t)
    tf = FFN_TF
    grid = (t // tm, D_FF // tf)
    casts = make_casts(*grid)
    in_specs = [
        pl.BlockSpec((tm, d), lambda i, f: (i, 0)),
        pl.BlockSpec((tm, d), lambda i, f: (i, 0)),
        pl.BlockSpec((1, d, tf), lambda i, f: (f, 0, 0)),
        pl.BlockSpec((1, d, tf), lambda i, f: (f, 0, 0)),
        pl.BlockSpec((tf, d), lambda i, f: (f, 0)),
    ]
    outs = pl.pallas_call(
        _with_casts(_ffn_kernel, len(in_specs), 1, len(casts)),
        grid=grid,
        in_specs=in_specs + [c.in_spec for c in casts],
        out_specs=[pl.BlockSpec((tm, d), lambda i, f: (i, 0))] + [c.out_spec for c in casts],
        out_shape=[jax.ShapeDtypeStruct((t, d), F32)] + [c.out_shape for c in casts],
        compiler_params=_cparams(("arbitrary", "arbitrary")),
        name="dense_swiglu",
    )(h, x, wg, wu, wd, *[c.array for c in casts])
    return outs[0], outs[1:]


def _moe_kernel(te_ref, tv_ref, src_ref, nxt_ref, x_hbm, g_ref, wg_ref, wu_ref, wd_ref, o_ref,
                xg, hs, sem):
    i = pl.program_id(0)
    f = pl.program_id(1)
    groups = xg.shape[1]
    tm, d = hs.shape
    per_step = groups // (D_FF // FFN_TF)
    slot = i % 2
    valid = tv_ref[i] == 1

    def start_group(idx_ref, dst_slot, grp):
        for sub in range(SUBLANES):
            tok = idx_ref[0, 0, grp * SUBLANES + sub]
            pltpu.make_async_copy(x_hbm.at[pl.ds(tok, 1)], xg.at[dst_slot, grp, pl.ds(sub, 1)],
                                  sem.at[dst_slot]).start()

    @pl.when((i == 0) & (f == 0))
    def _():
        def body(grp, carry):
            start_group(src_ref, 0, grp)
            return carry

        lax.fori_loop(0, groups, body, 0)

    @pl.when((f == 0) & ((i == 0) | (tv_ref[jnp.maximum(i - 1, 0)] == 1)))
    def _():
        pltpu.make_async_copy(xg.at[1 - slot], xg.at[slot], sem.at[slot]).wait()

    @pl.when(f == 0)
    def _():
        o_ref[...] = jnp.zeros(o_ref.shape, F32)

    @pl.when(valid & (f == 0))
    def _():
        rows = xg[slot, 0:tm // SUBLANES].reshape(tm, d)
        hs[...] = _rms(rows, g_ref[...]).astype(BF16)

    @pl.when(valid)
    def _():
        h = hs[...]
        gt = jnp.dot(h, wg_ref[0, 0], preferred_element_type=F32)
        up = jnp.dot(h, wu_ref[0, 0], preferred_element_type=F32)
        for k in range(per_step):
            start_group(nxt_ref, 1 - slot, f * per_step + k)
        act = (jax.nn.silu(gt) * up).astype(BF16)
        o_ref[...] += jnp.dot(act, wd_ref[0], preferred_element_type=F32)


def _gather_groups(tm):
    nf = D_FF // FFN_TF
    return nf * pl.cdiv(tm // SUBLANES, nf)


def _moe_ffn(x, g, tile_expert, tile_valid, src3, wg, wu, wd):
    t, d = x.shape
    n_tiles, _, width = src3.shape
    tm = min(FFN_TM, t)
    tf = FFN_TF
    nf = D_FF // tf

    def w_col(i, f, te, tv):
        return (te[i], jnp.where(tv[i] == 1, f, nf - 1), 0, 0)

    def w_row(i, f, te, tv):
        return (te[i], jnp.where(tv[i] == 1, f, nf - 1), 0)

    grid_spec = pltpu.PrefetchScalarGridSpec(
        num_scalar_prefetch=2,
        grid=(n_tiles, nf),
        in_specs=[
            pl.BlockSpec((1, 1, width), lambda i, f, te, tv: (i, 0, 0), memory_space=pltpu.SMEM),
            pl.BlockSpec((1, 1, width),
                         lambda i, f, te, tv: (jnp.minimum(i + 1, n_tiles - 1), 0, 0),
                         memory_space=pltpu.SMEM),
            pl.BlockSpec(memory_space=pl.ANY),
            pl.BlockSpec((1, d), lambda i, f, te, tv: (0, 0)),
            pl.BlockSpec((1, 1, d, tf), w_col),
            pl.BlockSpec((1, 1, d, tf), w_col),
            pl.BlockSpec((1, tf, d), w_row),
        ],
        out_specs=pl.BlockSpec((tm, d), lambda i, f, te, tv: (i, 0)),
        scratch_shapes=[
            pltpu.VMEM((2, _gather_groups(tm), SUBLANES, d), F32),
            pltpu.VMEM((tm, d), BF16),
            pltpu.SemaphoreType.DMA((2,)),
        ],
    )
    return pl.pallas_call(
        _moe_kernel,
        grid_spec=grid_spec,
        out_shape=jax.ShapeDtypeStruct((n_tiles * tm, d), F32),
        compiler_params=_cparams(("arbitrary", "arbitrary")),
        name="expert_swiglu",
    )(tile_expert, tile_valid, src3, src3, x, g, wg, wu, wd)


def _combine_kernel(pos_ref, nxt_ref, x_ref, gate_ref, g_ref, ys_hbm, o_ref, buf, sem):
    i = pl.program_id(0)
    tt = x_ref.shape[0]
    slot = i % 2

    def start_rows(idx_ref, dst_slot):
        def body(grp, carry):
            for sub in range(SUBLANES):
                for k in range(TOP_K):
                    p = idx_ref[0, 0, TOP_K * (grp * SUBLANES + sub) + k]
                    pltpu.make_async_copy(ys_hbm.at[pl.ds(p, 1)],
                                          buf.at[dst_slot, k, grp, pl.ds(sub, 1)],
                                          sem.at[dst_slot]).start()
            return carry

        lax.fori_loop(0, tt // SUBLANES, body, 0)

    @pl.when(i == 0)
    def _():
        start_rows(pos_ref, 0)

    pltpu.make_async_copy(buf.at[1 - slot], buf.at[slot], sem.at[slot]).wait()

    @pl.when(i + 1 < pl.num_programs(0))
    def _():
        start_rows(nxt_ref, 1 - slot)

    d = x_ref.shape[1]
    gates = gate_ref[...]
    moe = gates[:, 0:1] * buf[slot, 0].reshape(tt, d)
    for k in range(1, TOP_K):
        moe = moe + gates[:, k:k + 1] * buf[slot, k].reshape(tt, d)
    o_ref[...] = _rms(x_ref[...] + moe, g_ref[...])


def _combine(x, gates, pos3, ys, g):
    t, d = x.shape
    n_steps, _, per = pos3.shape
    tt = per // TOP_K
    return pl.pallas_call(
        _combine_kernel,
        grid=(n_steps,),
        in_specs=[
            pl.BlockSpec((1, 1, per), lambda i: (i, 0, 0), memory_space=pltpu.SMEM),
            pl.BlockSpec((1, 1, per), lambda i: (jnp.minimum(i + 1, n_steps - 1), 0, 0),
                         memory_space=pltpu.SMEM),
            pl.BlockSpec((tt, d), lambda i: (i, 0)),
            pl.BlockSpec((tt, LANES), lambda i: (i, 0)),
            pl.BlockSpec((1, d), lambda i: (0, 0)),
            pl.BlockSpec(memory_space=pl.ANY),
        ],
        out_specs=pl.BlockSpec((tt, d), lambda i: (i, 0)),
        out_shape=jax.ShapeDtypeStruct((t, d), F32),
        scratch_shapes=[pltpu.VMEM((2, TOP_K, tt // SUBLANES, SUBLANES, d), F32),
                        pltpu.SemaphoreType.DMA((2,))],
        compiler_params=_cparams(("arbitrary",)),
        name="combine_final_norm",
    )(pos3, pos3, x, gates, g, ys)


def _routing_plan(top_idx, tm):
    t = top_idx.shape[0]
    n_slots = t * TOP_K
    e_flat = top_idx.reshape(n_slots)
    onehot = (e_flat[:, None] == jnp.arange(N_EXPERTS, dtype=jnp.int32)[None, :]).astype(jnp.int32)
    csum = jnp.cumsum(onehot, axis=0)
    counts = csum[-1]
    rank = jnp.sum((csum - onehot) * onehot, axis=1)
    padded = ((counts + tm - 1) // tm) * tm
    ends = jnp.cumsum(padded)
    starts = ends - padded
    pos = jnp.sum(starts[None, :] * onehot, axis=1) + rank
    n_tiles = n_slots // tm + N_EXPERTS + 1
    src = jnp.zeros((n_tiles * tm,), jnp.int32).at[pos].set(
        jnp.arange(n_slots, dtype=jnp.int32) // TOP_K)
    width = LANES * pl.cdiv(_gather_groups(tm) * SUBLANES, LANES)
    src = jnp.pad(src.reshape(n_tiles, tm), ((0, 0), (0, width - tm)))
    tile_start = jnp.arange(n_tiles, dtype=jnp.int32) * tm
    tile_expert = jnp.minimum(
        jnp.sum((tile_start[:, None] >= ends[None, :]).astype(jnp.int32), axis=1), N_EXPERTS - 1)
    tile_valid = (tile_start < ends[-1]).astype(jnp.int32)
    return pos, src.reshape(n_tiles, 1, width), tile_expert, tile_valid


def kernel(x, norm1_g, w_in, conv_w, conv_b, lru_wa, lru_ba, lru_wx, lru_bx, lru_lambda,
           lru_norm_g, ret_norm_g, w_out, norm2_g, ffn_w_gate, ffn_w_up, ffn_w_down,
           moe_router, moe_w_gate, moe_w_up, moe_w_down, final_norm_g):
    assert DEPTH == 2 and w_in.shape[0] == DEPTH
    b, s, d = x.shape
    t = b * s
    tables = _ret_tables(s)
    xf = x.reshape(t, d)
    row = lambda v: v.reshape(1, -1)

    def branches(layer, x_in, w_in_bf16, lru_casts):
        proj = _norm_matmul(x_in, row(norm1_g[layer]), w_in_bf16)
        proj3 = proj.reshape(b, s, D_IN)
        y_lru, lru_cast = _lru_branch(proj3, conv_w[layer], row(conv_b[layer]),
                                      lru_wa[layer].astype(BF16), row(lru_ba[layer]),
                                      lru_wx[layer].astype(BF16), row(lru_bx[layer]),
                                      row(lru_lambda[layer]), row(lru_norm_g[layer]), lru_casts)
        y_ret = _ret_branch(proj3, tables, row(ret_norm_g[layer]))
        return y_lru.reshape(t, D_RNN), y_ret.reshape(t, D_RET), lru_cast

    flat2 = lambda w: w.reshape(-1, w.shape[-1])
    inner = s // min(SEQ_TS, s)
    n_steps = b * inner
    step = lambda i, j: i * inner + j
    yl, yr, (wg0, wu0, wd0, wo0, wi1, wo1) = branches(
        0, xf, w_in[0].astype(BF16),
        [_tiled_cast(ffn_w_gate[0], 1, n_steps, step), _tiled_cast(ffn_w_up[0], 1, n_steps, step),
         _flat_cast(ffn_w_down[0], n_steps, step), _flat_cast(w_out[0], n_steps, step),
         _flat_cast(w_in[1], n_steps, step), _flat_cast(w_out[1], n_steps, step)])
    x1, h2 = _out_proj(yl, yr, xf, wo0, row(norm2_g[0]))
    xf, (moe_g, moe_d) = _dense_ffn(
        h2, x1, wg0[0], wu0[0], wd0,
        lambda ni, nf: [_grid_tiled_cast(flat2(moe_w_gate[0]), N_EXPERTS, ni),
                        _flat_cast(flat2(moe_w_down[0]), ni * nf, lambda i, f: i * nf + f)])
    yl, yr, (moe_u,) = branches(1, xf, wi1,
                                [_tiled_cast(flat2(moe_w_up[0]), N_EXPERTS, n_steps, step)])
    router = jnp.pad(moe_router[0], ((0, 0), (0, LANES - N_EXPERTS))).astype(BF16)
    x1, idx, gates = _out_proj(yl, yr, xf, wo1, row(norm2_g[1]), router)
    tm = min(FFN_TM, t)
    pos, src3, tile_expert, tile_valid = _routing_plan(idx[:, :TOP_K], tm)
    ys = _moe_ffn(x1, row(norm2_g[1]), tile_expert, tile_valid, src3,
                  moe_g, moe_u, moe_d.reshape(moe_w_down.shape[1:]))
    tt = min(COMB_TT, t)
    pos3 = pos.reshape(t // tt, 1, tt * TOP_K)
    out = _combine(x1, gates, pos3, ys, row(final_norm_g))
    return out.reshape(b, s, d)
```

```python
import functools

import jax
import jax.numpy as jnp
from jax import lax
from jax.experimental import pallas as pl
from jax.experimental.pallas import tpu as pltpu

F32 = jnp.float32
BF16 = jnp.bfloat16

D_MODEL = 2048
DEPTH = 2
D_RNN = D_MODEL // 2
D_RET = D_MODEL - D_RNN
LRU_BLOCKS = 8
LRU_BLOCK = D_RNN // LRU_BLOCKS
CONV_WIDTH = 4
LRU_C = 8.0
RET_HEADS = 8
RET_DV = D_RET // RET_HEADS
RET_DK = RET_DV // 2
RET_CHUNK = 128
ROPE_BASE = 10000.0
D_FF = 256 * ((8 * D_MODEL // 3 + 255) // 256)
N_EXPERTS = 8
TOP_K = 2
EPS = 1e-6
LOG2_E = 1.4426950408889634
D_QK = RET_HEADS * RET_DK
D_IN = 2 * D_RNN + 2 * D_QK + 2 * D_RET

LANES = 128
SUBLANES = 8
VMEM_LIMIT_BYTES = 56 * 1024 * 1024

PROJ_TM = 1024
PROJ_TN = 1024
SEQ_TS = 256
OUT_TM = 512
FFN_TM = 512
FFN_TF = 512
COMB_TT = 256


def _cparams(sem):
    return pltpu.CompilerParams(dimension_semantics=sem, vmem_limit_bytes=VMEM_LIMIT_BYTES)


def _rms(x, g):
    return x * lax.rsqrt(jnp.mean(x * x, axis=-1, keepdims=True) + EPS) * g


def _with_casts(body, n_in, n_out, n_cast):
    def kern(*refs):
        ins = refs[:n_in]
        cast_in = refs[n_in:n_in + n_cast]
        outs = refs[n_in + n_cast:n_in + n_cast + n_out]
        cast_out = refs[n_in + n_cast + n_out:n_in + 2 * n_cast + n_out]
        scratch = refs[n_in + 2 * n_cast + n_out:]
        for src, dst in zip(cast_in, cast_out):
            _cast_block(src, dst)
        body(*ins, *outs, *scratch)

    return kern


def _cast_block(src, dst):
    if len(dst.shape) == 2:
        dst[...] = src[...].astype(BF16)
        return
    n, _, tf = dst.shape[-3:]
    lead = (0,) * (len(dst.shape) - 3)
    for f in range(n):
        dst[lead + (f,)] = src[:, f * tf:(f + 1) * tf].astype(BF16)


class _Cast:
    def __init__(self, array, in_spec, out_spec, out_shape):
        self.array, self.in_spec, self.out_spec = array, in_spec, out_spec
        self.out_shape = jax.ShapeDtypeStruct(out_shape, BF16)


def _flat_cast(a, n_steps, step_of):
    rows, cols = a.shape
    spec = pl.BlockSpec((rows // n_steps, cols), lambda *ids: (step_of(*ids), 0))
    return _Cast(a, spec, spec, (rows, cols))


def _tiled_cast(a, n_experts, n_steps, step_of):
    rows_all, cols = a.shape
    rows = rows_all // n_experts
    blk = rows_all // n_steps
    per = rows // blk
    in_spec = pl.BlockSpec((blk, cols), lambda *ids: (step_of(*ids), 0))
    out_spec = pl.BlockSpec((1, cols // FFN_TF, blk, FFN_TF),
                            lambda *ids: (step_of(*ids) // per, 0, step_of(*ids) % per, 0))
    return _Cast(a, in_spec, out_spec, (n_experts, cols // FFN_TF, rows, FFN_TF))


def _grid_tiled_cast(a, n_experts, n_row_steps):
    rows_all, cols = a.shape
    rows = rows_all // n_experts
    blk = rows_all // n_row_steps
    per = rows // blk
    in_spec = pl.BlockSpec((blk, FFN_TF), lambda i, f: (i, f))
    out_spec = pl.BlockSpec((1, 1, blk, FFN_TF), lambda i, f: (i // per, f, i % per, 0))
    return _Cast(a, in_spec, out_spec, (n_experts, cols // FFN_TF, rows, FFN_TF))


def _norm_matmul_kernel(x_ref, g_ref, w_ref, o_ref, h_ref):
    @pl.when(pl.program_id(1) == 0)
    def _():
        h_ref[...] = _rms(x_ref[...], g_ref[...]).astype(BF16)

    o_ref[...] = jnp.dot(h_ref[...], w_ref[...], preferred_element_type=F32)


def _norm_matmul(x, g, w):
    t, d = x.shape
    n = w.shape[1]
    tm = min(PROJ_TM, t)
    tn = PROJ_TN
    return pl.pallas_call(
        _norm_matmul_kernel,
        grid=(t // tm, n // tn),
        in_specs=[
            pl.BlockSpec((tm, d), lambda i, j: (i, 0)),
            pl.BlockSpec((1, d), lambda i, j: (0, 0)),
            pl.BlockSpec((d, tn), lambda i, j: (0, j)),
        ],
        out_specs=pl.BlockSpec((tm, tn), lambda i, j: (i, j)),
        out_shape=jax.ShapeDtypeStruct((t, n), F32),
        scratch_shapes=[pltpu.VMEM((tm, d), BF16)],
        compiler_params=_cparams(("parallel", "arbitrary")),
        name="norm_in_proj",
    )(x, g, w)


def _lru_kernel(x_ref, gate_ref, cw_ref, cb_ref, wa_ref, ba_ref, wx_ref, bx_ref, lam_ref,
                ng_ref, o_ref, xbuf, a_s, b_s, hcar):
    c = pl.program_id(1)
    ts = x_ref.shape[1]
    hist = SUBLANES

    @pl.when(c == 0)
    def _():
        xbuf[...] = jnp.zeros((hist, D_RNN), F32)
        hcar[...] = jnp.zeros((1, D_RNN), F32)

    x = x_ref[0]
    prev = xbuf[...]
    rows8 = lax.broadcasted_iota(jnp.int32, (SUBLANES, D_RNN), 0)
    xc = cb_ref[...]
    for tap in range(CONV_WIDTH):
        back = CONV_WIDTH - 1 - tap
        if back == 0:
            xs = x
        else:
            rolled = pltpu.roll(x, back, axis=0)
            head = jnp.where(rows8 < back, pltpu.roll(prev, back, axis=0), rolled[0:hist, :])
            xs = jnp.concatenate([head, rolled[hist:, :]], axis=0)
        xc = xc + xs * cw_ref[tap:tap + 1, :]
    xbuf[...] = x[ts - hist:ts, :]

    xcb = xc.astype(BF16)
    ra = []
    rx = []
    for n in range(LRU_BLOCKS):
        xs = xcb[:, n * LRU_BLOCK:(n + 1) * LRU_BLOCK]
        ra.append(jnp.dot(xs, wa_ref[n], preferred_element_type=F32))
        rx.append(jnp.dot(xs, wx_ref[n], preferred_element_type=F32))
    r = jax.nn.sigmoid(jnp.concatenate(ra, axis=1) + ba_ref[...])
    ig = jax.nn.sigmoid(jnp.concatenate(rx, axis=1) + bx_ref[...])

    z = -lam_ref[...]
    softplus = jnp.maximum(z, 0.0) + jnp.log1p(jnp.exp(-jnp.abs(z)))
    a = jnp.exp2(r * ((-LRU_C * LOG2_E) * softplus))
    om = 1.0 - a * a
    mult = jnp.where(om > 0.0, om * lax.rsqrt(om), 0.0)
    a_s[...] = a
    b_s[...] = mult * ig * xc

    @pl.when(c == 0)
    def _():
        b_s[0:1, :] = ig[0:1, :] * xc[0:1, :]

    def body(i, h):
        r0 = pl.multiple_of(i * SUBLANES, SUBLANES)
        av = a_s[pl.ds(r0, SUBLANES), :]
        bv = b_s[pl.ds(r0, SUBLANES), :]
        for sh in (1, 2, 4):
            a_sh = pltpu.roll(av, sh, axis=0)
            b_sh = pltpu.roll(bv, sh, axis=0)
            m = rows8 >= sh
            bv = jnp.where(m, av * b_sh + bv, bv)
            av = jnp.where(m, av * a_sh, av)
        hv = av * h + bv
        b_s[pl.ds(r0, SUBLANES), :] = hv
        return hv[SUBLANES - 1:SUBLANES, :]

    hcar[...] = lax.fori_loop(0, ts // SUBLANES, body, hcar[...])

    y = _rms(b_s[...], ng_ref[...])
    o_ref[0] = (y * jax.nn.gelu(gate_ref[0])).astype(BF16)


def _lru_branch(proj3, cw, cb, wa, ba, wx, bx, lam, ng, casts):
    b, s, _ = proj3.shape
    ts = min(SEQ_TS, s)
    vec = pl.BlockSpec((1, D_RNN), lambda i, j: (0, 0))
    blk = pl.BlockSpec((LRU_BLOCKS, LRU_BLOCK, LRU_BLOCK), lambda i, j: (0, 0, 0))
    in_specs = [
        pl.BlockSpec((1, ts, D_RNN), lambda i, j: (i, j, 0)),
        pl.BlockSpec((1, ts, D_RNN), lambda i, j: (i, j, 1)),
        pl.BlockSpec((CONV_WIDTH, D_RNN), lambda i, j: (0, 0)),
        vec, blk, vec, blk, vec, vec, vec,
    ]
    outs = pl.pallas_call(
        _with_casts(_lru_kernel, len(in_specs), 1, len(casts)),
        grid=(b, s // ts),
        in_specs=in_specs + [c.in_spec for c in casts],
        out_specs=[pl.BlockSpec((1, ts, D_RNN), lambda i, j: (i, j, 0))]
        + [c.out_spec for c in casts],
        out_shape=[jax.ShapeDtypeStruct((b, s, D_RNN), BF16)] + [c.out_shape for c in casts],
        scratch_shapes=[
            pltpu.VMEM((SUBLANES, D_RNN), F32),
            pltpu.VMEM((ts, D_RNN), F32),
            pltpu.VMEM((ts, D_RNN), F32),
            pltpu.VMEM((1, D_RNN), F32),
        ],
        compiler_params=_cparams(("arbitrary", "arbitrary")),
        name="rglru_branch",
    )(proj3, proj3, cw, cb, wa, ba, wx, bx, lam, ng, *[c.array for c in casts])
    return outs[0], outs[1:]


def _rope(x, cos, sin_signed):
    half = RET_DK // 2
    lane = lax.broadcasted_iota(jnp.int32, x.shape, 1)
    first = (lane & (RET_DK - 1)) < half
    rot = jnp.where(first, pltpu.roll(x, LANES - half, axis=1), pltpu.roll(x, half, axis=1))
    return x * cos + rot * sin_signed


def _ret_kernel(q_ref, k_ref, v_ref, g_ref, cos_ref, sin_ref, xim_ref, zmt_ref, dec_ref,
                cd_ref, gn_ref, o_ref, state):
    c = pl.program_id(1)
    ts = q_ref.shape[1]

    @pl.when(c == 0)
    def _():
        state[...] = jnp.zeros(state.shape, F32)

    lane = lax.broadcasted_iota(jnp.int32, (RET_CHUNK, LANES), 1)
    even_head = lane < RET_DK
    for p in range(RET_HEADS // 2):
        ps = slice(p * LANES, (p + 1) * LANES)
        for n in range(ts // RET_CHUNK):
            rows = slice(n * RET_CHUNK, (n + 1) * RET_CHUNK)
            cos = cos_ref[rows, :]
            sin = sin_ref[rows, :]
            qp = _rope(q_ref[0, rows, ps], cos, sin)
            kt = _rope(k_ref[0, rows, ps], cos, sin).T
            ktb = kt.astype(BF16)
            qm = jnp.concatenate([jnp.where(even_head, qp, 0.0), jnp.where(even_head, 0.0, qp)],
                                 axis=0).astype(BF16)
            sc = jnp.dot(qm, ktb, preferred_element_type=F32)
            for par in range(2):
                h = 2 * p + par
                vs = slice(h * RET_DV, (h + 1) * RET_DV)
                vh = v_ref[0, rows, vs].astype(BF16)
                scores = sc[par * RET_CHUNK:(par + 1) * RET_CHUNK, :] * dec_ref[h]
                inner = jnp.dot(scores.astype(BF16), vh, preferred_element_type=F32)
                st = state[h]
                cross = jnp.dot((qp * xim_ref[h]).astype(BF16), st.astype(BF16),
                                preferred_element_type=F32)
                kv = jnp.dot((kt * zmt_ref[h]).astype(BF16), vh, preferred_element_type=F32)
                state[h] = cd_ref[h] * st + kv
                o = inner + cross
                mu = jnp.mean(o, axis=-1, keepdims=True)
                d = o - mu
                var = jnp.mean(d * d, axis=-1, keepdims=True)
                y = d * lax.rsqrt(var + EPS) * gn_ref[:, vs]
                o_ref[0, rows, vs] = (y * jax.nn.silu(g_ref[0, rows, vs])).astype(BF16)


def _ret_tables(s):
    scale = RET_DK ** -0.5
    inv_freq = ROPE_BASE ** (-jnp.arange(0, RET_DK, 2, dtype=F32) / RET_DK)
    ang = jnp.arange(s, dtype=F32)[:, None] * inv_freq[None, :]
    cos = jnp.cos(ang)
    sin = jnp.sin(ang)
    cos_t = jnp.tile(cos, (1, 2 * LANES // RET_DK))
    sin_t = jnp.tile(jnp.concatenate([-sin, sin], axis=1), (1, LANES // RET_DK))
    log_g = jnp.log(1.0 - 2.0 ** (-5.0 - jnp.arange(RET_HEADS, dtype=F32)))
    idx = jnp.arange(RET_CHUNK, dtype=F32)
    rel = idx[:, None] - idx[None, :]
    decay = jnp.where(rel >= 0, jnp.exp(jnp.maximum(rel, 0.0)[None] * log_g[:, None, None]), 0.0)
    zeta = jnp.exp((RET_CHUNK - 1.0 - idx)[None, :] * log_g[:, None])
    xi = jnp.exp((idx + 1.0)[None, :] * log_g[:, None])
    own = (jnp.arange(LANES)[None, :] // RET_DK) == (jnp.arange(RET_HEADS)[:, None] % 2)
    xim = jnp.where(own[:, None, :], xi[:, :, None], 0.0)
    zmt = jnp.where(own[:, :, None], scale * zeta[:, None, :], 0.0)
    cdec = jnp.exp(RET_CHUNK * log_g)
    return cos_t, sin_t, xim, zmt, decay * scale, cdec


def _ret_branch(proj3, tables, gn):
    b, s, _ = proj3.shape
    ts = min(SEQ_TS, s)
    cos_t, sin_t, xim, zmt, decay, cdec = tables
    q_blk = 2 * D_RNN // D_QK
    per_head = pl.BlockSpec((RET_HEADS, RET_CHUNK, RET_CHUNK), lambda i, j: (0, 0, 0))
    assert RET_CHUNK == LANES == RET_DV == 2 * RET_DK
    return pl.pallas_call(
        _ret_kernel,
        grid=(b, s // ts),
        in_specs=[
            pl.BlockSpec((1, ts, D_QK), lambda i, j: (i, j, q_blk)),
            pl.BlockSpec((1, ts, D_QK), lambda i, j: (i, j, q_blk + 1)),
            pl.BlockSpec((1, ts, D_RET), lambda i, j: (i, j, 3)),
            pl.BlockSpec((1, ts, D_RET), lambda i, j: (i, j, 4)),
            pl.BlockSpec((ts, LANES), lambda i, j: (j, 0)),
            pl.BlockSpec((ts, LANES), lambda i, j: (j, 0)),
            per_head, per_head, per_head,
            pl.BlockSpec(memory_space=pltpu.SMEM),
            pl.BlockSpec((1, D_RET), lambda i, j: (0, 0)),
        ],
        out_specs=pl.BlockSpec((1, ts, D_RET), lambda i, j: (i, j, 0)),
        out_shape=jax.ShapeDtypeStruct((b, s, D_RET), BF16),
        scratch_shapes=[pltpu.VMEM((RET_HEADS, LANES, RET_DV), F32)],
        compiler_params=_cparams(("parallel", "arbitrary")),
        name="retention_branch",
    )(proj3, proj3, proj3, proj3, cos_t, sin_t, xim, zmt, decay, cdec, gn)


def _out_proj_kernel(yl_ref, yr_ref, x_ref, w_ref, g_ref, x1_ref, h_ref):
    acc = jnp.dot(yl_ref[...], w_ref[0:D_RNN, :], preferred_element_type=F32)
    acc = acc + jnp.dot(yr_ref[...], w_ref[D_RNN:D_MODEL, :], preferred_element_type=F32)
    x1 = x_ref[...] + acc
    x1_ref[...] = x1
    h_ref[...] = _rms(x1, g_ref[...]).astype(BF16)


def _out_proj_router_kernel(yl_ref, yr_ref, x_ref, w_ref, g_ref, rw_ref, x1_ref, hp_ref, idx_ref,
                            gate_ref):
    acc = jnp.dot(yl_ref[...], w_ref[0:D_RNN, :], preferred_element_type=F32)
    acc = acc + jnp.dot(yr_ref[...], w_ref[D_RNN:D_MODEL, :], preferred_element_type=F32)
    x1 = x_ref[...] + acc
    x1_ref[...] = x1
    h = _rms(x1, g_ref[...]).astype(BF16)
    hf = h.astype(F32)
    for a in range(hp_ref.shape[1]):
        hp_ref[:, a, :] = hf[:, a * LANES:(a + 1) * LANES]
    logits = jnp.dot(h, rw_ref[...], preferred_element_type=F32)
    lane_i = lax.broadcasted_iota(jnp.int32, logits.shape, 1)
    lane = lane_i.astype(F32)
    neg = jnp.float32(-jnp.inf)
    logits = jnp.where(lane_i < N_EXPERTS, logits, neg)
    m1 = jnp.max(logits, axis=-1, keepdims=True)
    i1 = jnp.min(jnp.where(logits == m1, lane, float(LANES)), axis=-1, keepdims=True)
    rest = jnp.where(lane == i1, neg, logits)
    m2 = jnp.max(rest, axis=-1, keepdims=True)
    i2 = jnp.min(jnp.where(rest == m2, lane, float(LANES)), axis=-1, keepdims=True)
    e2 = jnp.exp(m2 - m1)
    den = 1.0 + e2
    idx_ref[...] = jnp.where(lane_i == 0, i1, i2).astype(jnp.int32)
    gate_ref[...] = jnp.where(lane_i == 0, 1.0 / den, e2 / den)


def _out_proj(yl, yr, x, w, g, router=None):
    t, d = x.shape
    tm = min(OUT_TM, t)
    row = lambda i: (i, 0)
    fixed = lambda i: (0, 0)
    in_specs = [
        pl.BlockSpec((tm, D_RNN), row),
        pl.BlockSpec((tm, D_RET), row),
        pl.BlockSpec((tm, d), row),
        pl.BlockSpec((d, d), fixed),
        pl.BlockSpec((1, d), fixed),
    ]
    if router is None:
        return pl.pallas_call(
            _out_proj_kernel,
            grid=(t // tm,),
            in_specs=in_specs,
            out_specs=[pl.BlockSpec((tm, d), row), pl.BlockSpec((tm, d), row)],
            out_shape=[jax.ShapeDtypeStruct((t, d), F32), jax.ShapeDtypeStruct((t, d), BF16)],
            compiler_params=_cparams(("parallel",)),
            name="out_proj_norm",
        )(yl, yr, x, w, g)
    return pl.pallas_call(
        _out_proj_router_kernel,
        grid=(t // tm,),
        in_specs=in_specs + [pl.BlockSpec((d, LANES), fixed)],
        out_specs=[pl.BlockSpec((tm, d), row),
                   pl.BlockSpec((tm, d // LANES, LANES), lambda i: (i, 0, 0)),
                   pl.BlockSpec((tm, LANES), row), pl.BlockSpec((tm, LANES), row)],
        out_shape=[jax.ShapeDtypeStruct((t, d), F32),
                   jax.ShapeDtypeStruct((t, d // LANES, LANES), F32),
                   jax.ShapeDtypeStruct((t, LANES), jnp.int32),
                   jax.ShapeDtypeStruct((t, LANES), F32)],
        compiler_params=_cparams(("parallel",)),
        name="out_proj_router",
    )(yl, yr, x, w, g, router)


def _swiglu_step(h, wg, wu, wd):
    gt = jnp.dot(h, wg, preferred_element_type=F32)
    up = jnp.dot(h, wu, preferred_element_type=F32)
    act = (jax.nn.silu(gt) * up).astype(BF16)
    return jnp.dot(act, wd, preferred_element_type=F32)


def _ffn_kernel(h_ref, x_ref, wg_ref, wu_ref, wd_ref, o_ref):
    @pl.when(pl.program_id(1) == 0)
    def _():
        o_ref[...] = x_ref[...]

    o_ref[...] += _swiglu_step(h_ref[...], wg_ref[0], wu_ref[0], wd_ref[...])


def _dense_ffn(h, x, wg, wu, wd, make_casts):
    t, d = x.shape
    tm = min(FFN_TM, t)
    tf = FFN_TF
    grid = (t // tm, D_FF // tf)
    casts = make_casts(*grid)
    in_specs = [
        pl.BlockSpec((tm, d), lambda i, f: (i, 0)),
        pl.BlockSpec((tm, d), lambda i, f: (i, 0)),
        pl.BlockSpec((1, d, tf), lambda i, f: (f, 0, 0)),
        pl.BlockSpec((1, d, tf), lambda i, f: (f, 0, 0)),
        pl.BlockSpec((tf, d), lambda i, f: (f, 0)),
    ]
    outs = pl.pallas_call(
        _with_casts(_ffn_kernel, len(in_specs), 1, len(casts)),
        grid=grid,
        in_specs=in_specs + [c.in_spec for c in casts],
        out_specs=[pl.BlockSpec((tm, d), lambda i, f: (i, 0))] + [c.out_spec for c in casts],
        out_shape=[jax.ShapeDtypeStruct((t, d), F32)] + [c.out_shape for c in casts],
        compiler_params=_cparams(("arbitrary", "arbitrary")),
        name="dense_swiglu",
    )(h, x, wg, wu, wd, *[c.array for c in casts])
    return outs[0], outs[1:]


def _moe_kernel(te_ref, tv_ref, src_ref, nxt_ref, hp_hbm, wg_ref, wu_ref, wd_ref, o_ref,
                xg, wide, hs, sem):
    i = pl.program_id(0)
    f = pl.program_id(1)
    rows = xg.shape[1]
    tm = hs.shape[0]
    per_step = rows // (D_FF // FFN_TF)
    slot = i % 2
    valid = tv_ref[i] == 1

    def start_row(idx_ref, dst_slot, r):
        tok = idx_ref[0, 0, r]
        pltpu.make_async_copy(hp_hbm.at[tok], xg.at[dst_slot, r], sem.at[dst_slot]).start()

    @pl.when((i == 0) & (f == 0))
    def _():
        def body(r, carry):
            start_row(src_ref, 0, r)
            return carry

        lax.fori_loop(0, rows, body, 0, unroll=8)

    @pl.when((f == 0) & ((i == 0) | (tv_ref[jnp.maximum(i - 1, 0)] == 1)))
    def _():
        pltpu.make_async_copy(xg.at[1 - slot], xg.at[slot], sem.at[slot]).wait()

    @pl.when(f == 0)
    def _():
        o_ref[...] = jnp.zeros(o_ref.shape, F32)

    @pl.when(valid & (f == 0))
    def _():
        for a in range(xg.shape[2]):
            wide[:, a * LANES:(a + 1) * LANES] = xg[slot, 0:tm, a, :]
        hs[...] = wide[...].astype(BF16)

    @pl.when(valid)
    def _():
        h = hs[...]
        gt = jnp.dot(h, wg_ref[0, 0], preferred_element_type=F32)
        up = jnp.dot(h, wu_ref[0, 0], preferred_element_type=F32)
        for k in range(per_step):
            start_row(nxt_ref, 1 - slot, f * per_step + k)
        act = (jax.nn.silu(gt) * up).astype(BF16)
        o_ref[...] += jnp.dot(act, wd_ref[0], preferred_element_type=F32)


def _gather_rows(tm):
    nf = D_FF // FFN_TF
    return nf * pl.cdiv(tm, nf)


def _moe_ffn(hp, tile_expert, tile_valid, src3, wg, wu, wd):
    t = hp.shape[0]
    d = wd.shape[2]
    n_tiles, _, width = src3.shape
    tm = min(FFN_TM, t)
    tf = FFN_TF
    nf = D_FF // tf

    def w_col(i, f, te, tv):
        return (te[i], jnp.where(tv[i] == 1, f, nf - 1), 0, 0)

    def w_row(i, f, te, tv):
        return (te[i], jnp.where(tv[i] == 1, f, nf - 1), 0)

    grid_spec = pltpu.PrefetchScalarGridSpec(
        num_scalar_prefetch=2,
        grid=(n_tiles, nf),
        in_specs=[
            pl.BlockSpec((1, 1, width), lambda i, f, te, tv: (i, 0, 0), memory_space=pltpu.SMEM),
            pl.BlockSpec((1, 1, width),
                         lambda i, f, te, tv: (jnp.minimum(i + 1, n_tiles - 1), 0, 0),
                         memory_space=pltpu.SMEM),
            pl.BlockSpec(memory_space=pl.ANY),
            pl.BlockSpec((1, 1, d, tf), w_col),
            pl.BlockSpec((1, 1, d, tf), w_col),
            pl.BlockSpec((1, tf, d), w_row),
        ],
        out_specs=pl.BlockSpec((tm, d), lambda i, f, te, tv: (i, 0)),
        scratch_shapes=[
            pltpu.VMEM((2, _gather_rows(tm), d // LANES, LANES), F32),
            pltpu.VMEM((tm, d), F32),
            pltpu.VMEM((tm, d), BF16),
            pltpu.SemaphoreType.DMA((2,)),
        ],
    )
    return pl.pallas_call(
        _moe_kernel,
        grid_spec=grid_spec,
        out_shape=jax.ShapeDtypeStruct((n_tiles * tm, d), F32),
        compiler_params=_cparams(("arbitrary", "arbitrary")),
        name="expert_swiglu",
    )(tile_expert, tile_valid, src3, src3, hp, wg, wu, wd)


def _combine_kernel(pos_ref, nxt_ref, x_ref, gate_ref, g_ref, ys_hbm, o_ref, buf, sem):
    i = pl.program_id(0)
    tt = x_ref.shape[0]
    slot = i % 2

    def start_rows(idx_ref, dst_slot):
        def body(grp, carry):
            for sub in range(SUBLANES):
                for k in range(TOP_K):
                    p = idx_ref[0, 0, TOP_K * (grp * SUBLANES + sub) + k]
                    pltpu.make_async_copy(ys_hbm.at[pl.ds(p, 1)],
                                          buf.at[dst_slot, k, grp, pl.ds(sub, 1)],
                                          sem.at[dst_slot]).start()
            return carry

        lax.fori_loop(0, tt // SUBLANES, body, 0)

    @pl.when(i == 0)
    def _():
        start_rows(pos_ref, 0)

    pltpu.make_async_copy(buf.at[1 - slot], buf.at[slot], sem.at[slot]).wait()

    @pl.when(i + 1 < pl.num_programs(0))
    def _():
        start_rows(nxt_ref, 1 - slot)

    d = x_ref.shape[1]
    gates = gate_ref[...]
    moe = gates[:, 0:1] * buf[slot, 0].reshape(tt, d)
    for k in range(1, TOP_K):
        moe = moe + gates[:, k:k + 1] * buf[slot, k].reshape(tt, d)
    o_ref[...] = _rms(x_ref[...] + moe, g_ref[...])


def _combine(x, gates, pos3, ys, g):
    t, d = x.shape
    n_steps, _, per = pos3.shape
    tt = per // TOP_K
    return pl.pallas_call(
        _combine_kernel,
        grid=(n_steps,),
        in_specs=[
            pl.BlockSpec((1, 1, per), lambda i: (i, 0, 0), memory_space=pltpu.SMEM),
            pl.BlockSpec((1, 1, per), lambda i: (jnp.minimum(i + 1, n_steps - 1), 0, 0),
                         memory_space=pltpu.SMEM),
            pl.BlockSpec((tt, d), lambda i: (i, 0)),
            pl.BlockSpec((tt, LANES), lambda i: (i, 0)),
            pl.BlockSpec((1, d), lambda i: (0, 0)),
            pl.BlockSpec(memory_space=pl.ANY),
        ],
        out_specs=pl.BlockSpec((tt, d), lambda i: (i, 0)),
        out_shape=jax.ShapeDtypeStruct((t, d), F32),
        scratch_shapes=[pltpu.VMEM((2, TOP_K, tt // SUBLANES, SUBLANES, d), F32),
                        pltpu.SemaphoreType.DMA((2,))],
        compiler_params=_cparams(("arbitrary",)),
        name="combine_final_norm",
    )(pos3, pos3, x, gates, g, ys)


def _routing_plan(top_idx, tm):
    t = top_idx.shape[0]
    n_slots = t * TOP_K
    e_flat = top_idx.reshape(n_slots)
    onehot = (e_flat[:, None] == jnp.arange(N_EXPERTS, dtype=jnp.int32)[None, :]).astype(jnp.int32)
    csum = jnp.cumsum(onehot, axis=0)
    counts = csum[-1]
    rank = jnp.sum((csum - onehot) * onehot, axis=1)
    padded = ((counts + tm - 1) // tm) * tm
    ends = jnp.cumsum(padded)
    starts = ends - padded
    pos = jnp.sum(starts[None, :] * onehot, axis=1) + rank
    n_tiles = n_slots // tm + N_EXPERTS + 1
    src = jnp.zeros((n_tiles * tm,), jnp.int32).at[pos].set(
        jnp.arange(n_slots, dtype=jnp.int32) // TOP_K)
    width = LANES * pl.cdiv(_gather_rows(tm), LANES)
    src = jnp.pad(src.reshape(n_tiles, tm), ((0, 0), (0, width - tm)))
    tile_start = jnp.arange(n_tiles, dtype=jnp.int32) * tm
    tile_expert = jnp.minimum(
        jnp.sum((tile_start[:, None] >= ends[None, :]).astype(jnp.int32), axis=1), N_EXPERTS - 1)
    tile_valid = (tile_start < ends[-1]).astype(jnp.int32)
    return pos, src.reshape(n_tiles, 1, width), tile_expert, tile_valid


def kernel(x, norm1_g, w_in, conv_w, conv_b, lru_wa, lru_ba, lru_wx, lru_bx, lru_lambda,
           lru_norm_g, ret_norm_g, w_out, norm2_g, ffn_w_gate, ffn_w_up, ffn_w_down,
           moe_router, moe_w_gate, moe_w_up, moe_w_down, final_norm_g):
    assert DEPTH == 2 and w_in.shape[0] == DEPTH
    b, s, d = x.shape
    t = b * s
    tables = _ret_tables(s)
    xf = x.reshape(t, d)
    row = lambda v: v.reshape(1, -1)

    def branches(layer, x_in, w_in_bf16, lru_casts):
        proj = _norm_matmul(x_in, row(norm1_g[layer]), w_in_bf16)
        proj3 = proj.reshape(b, s, D_IN)
        y_lru, lru_cast = _lru_branch(proj3, conv_w[layer], row(conv_b[layer]),
                                      lru_wa[layer].astype(BF16), row(lru_ba[layer]),
                                      lru_wx[layer].astype(BF16), row(lru_bx[layer]),
                                      row(lru_lambda[layer]), row(lru_norm_g[layer]), lru_casts)
        y_ret = _ret_branch(proj3, tables, row(ret_norm_g[layer]))
        return y_lru.reshape(t, D_RNN), y_ret.reshape(t, D_RET), lru_cast

    flat2 = lambda w: w.reshape(-1, w.shape[-1])
    inner = s // min(SEQ_TS, s)
    n_steps = b * inner
    step = lambda i, j: i * inner + j
    yl, yr, (wg0, wu0, wd0, wo0, wi1, wo1) = branches(
        0, xf, w_in[0].astype(BF16),
        [_tiled_cast(ffn_w_gate[0], 1, n_steps, step), _tiled_cast(ffn_w_up[0], 1, n_steps, step),
         _flat_cast(ffn_w_down[0], n_steps, step), _flat_cast(w_out[0], n_steps, step),
         _flat_cast(w_in[1], n_steps, step), _flat_cast(w_out[1], n_steps, step)])
    x1, h2 = _out_proj(yl, yr, xf, wo0, row(norm2_g[0]))
    xf, (moe_g, moe_d) = _dense_ffn(
        h2, x1, wg0[0], wu0[0], wd0,
        lambda ni, nf: [_grid_tiled_cast(flat2(moe_w_gate[0]), N_EXPERTS, ni),
                        _flat_cast(flat2(moe_w_down[0]), ni * nf, lambda i, f: i * nf + f)])
    yl, yr, (moe_u,) = branches(1, xf, wi1,
                                [_tiled_cast(flat2(moe_w_up[0]), N_EXPERTS, n_steps, step)])
    router = jnp.pad(moe_router[0], ((0, 0), (0, LANES - N_EXPERTS))).astype(BF16)
    x1, hp, idx, gates = _out_proj(yl, yr, xf, wo1, row(norm2_g[1]), router)
    tm = min(FFN_TM, t)
    pos, src3, tile_expert, tile_valid = _routing_plan(idx[:, :TOP_K], tm)
    ys = _moe_ffn(hp, tile_expert, tile_valid, src3,
                  moe_g, moe_u, moe_d.reshape(moe_w_down.shape[1:]))
    tt = min(COMB_TT, t)
    pos3 = pos.reshape(t // tt, 1, tt * TOP_K)
    out = _combine(x1, gates, pos3, ys, row(final_norm_g))
    return out.reshape(b, s, d)
```

```python
import functools

import jax
import jax.numpy as jnp
from jax import lax
from jax.experimental import pallas as pl
from jax.experimental.pallas import tpu as pltpu

F32 = jnp.float32
BF16 = jnp.bfloat16

D_MODEL = 2048
DEPTH = 2
D_RNN = D_MODEL // 2
D_RET = D_MODEL - D_RNN
LRU_BLOCKS = 8
LRU_BLOCK = D_RNN // LRU_BLOCKS
CONV_WIDTH = 4
LRU_C = 8.0
RET_HEADS = 8
RET_DV = D_RET // RET_HEADS
RET_DK = RET_DV // 2
RET_CHUNK = 128
ROPE_BASE = 10000.0
D_FF = 256 * ((8 * D_MODEL // 3 + 255) // 256)
N_EXPERTS = 8
TOP_K = 2
EPS = 1e-6
LOG2_E = 1.4426950408889634
D_QK = RET_HEADS * RET_DK
D_IN = 2 * D_RNN + 2 * D_QK + 2 * D_RET

LANES = 128
SUBLANES = 8
VMEM_LIMIT_BYTES = 56 * 1024 * 1024

PROJ_TM = 1024
PROJ_TN = 1024
SEQ_TS = 256
OUT_TM = 512
FFN_TM = 512
FFN_TF = 512
COMB_TT = 256


def _cparams(sem):
    return pltpu.CompilerParams(dimension_semantics=sem, vmem_limit_bytes=VMEM_LIMIT_BYTES)


def _rms(x, g):
    return x * lax.rsqrt(jnp.mean(x * x, axis=-1, keepdims=True) + EPS) * g


def _with_casts(body, n_in, n_out, n_cast):
    def kern(*refs):
        ins = refs[:n_in]
        cast_in = refs[n_in:n_in + n_cast]
        outs = refs[n_in + n_cast:n_in + n_cast + n_out]
        cast_out = refs[n_in + n_cast + n_out:n_in + 2 * n_cast + n_out]
        scratch = refs[n_in + 2 * n_cast + n_out:]
        for src, dst in zip(cast_in, cast_out):
            _cast_block(src, dst)
        body(*ins, *outs, *scratch)

    return kern


def _cast_block(src, dst):
    if len(dst.shape) == 2:
        dst[...] = src[...].astype(BF16)
        return
    n, _, tf = dst.shape[-3:]
    lead = (0,) * (len(dst.shape) - 3)
    for f in range(n):
        dst[lead + (f,)] = src[:, f * tf:(f + 1) * tf].astype(BF16)


class _Cast:
    def __init__(self, array, in_spec, out_spec, out_shape):
        self.array, self.in_spec, self.out_spec = array, in_spec, out_spec
        self.out_shape = jax.ShapeDtypeStruct(out_shape, BF16)


def _flat_cast(a, n_steps, step_of, part=0, n_parts=1):
    rows_all, cols = a.shape
    rows = rows_all // n_parts
    blk = rows // n_steps
    in_spec = pl.BlockSpec((blk, cols), lambda *ids: (part * n_steps + step_of(*ids), 0))
    out_spec = pl.BlockSpec((blk, cols), lambda *ids: (step_of(*ids), 0))
    return _Cast(a, in_spec, out_spec, (rows, cols))


def _tiled_cast(a, n_experts, n_steps, step_of):
    rows_all, cols = a.shape
    rows = rows_all // n_experts
    blk = rows_all // n_steps
    per = rows // blk
    in_spec = pl.BlockSpec((blk, cols), lambda *ids: (step_of(*ids), 0))
    out_spec = pl.BlockSpec((1, cols // FFN_TF, blk, FFN_TF),
                            lambda *ids: (step_of(*ids) // per, 0, step_of(*ids) % per, 0))
    return _Cast(a, in_spec, out_spec, (n_experts, cols // FFN_TF, rows, FFN_TF))


def _grid_tiled_cast(a, n_experts, n_row_steps):
    rows_all, cols = a.shape
    rows = rows_all // n_experts
    blk = rows_all // n_row_steps
    per = rows // blk
    in_spec = pl.BlockSpec((blk, FFN_TF), lambda i, f: (i, f))
    out_spec = pl.BlockSpec((1, 1, blk, FFN_TF), lambda i, f: (i // per, f, i % per, 0))
    return _Cast(a, in_spec, out_spec, (n_experts, cols // FFN_TF, rows, FFN_TF))


def _norm_matmul_kernel(x_ref, g_ref, w_ref, o_ref, h_ref):
    @pl.when(pl.program_id(1) == 0)
    def _():
        h_ref[...] = _rms(x_ref[...], g_ref[...]).astype(BF16)

    o_ref[...] = jnp.dot(h_ref[...], w_ref[...], preferred_element_type=F32)


def _norm_matmul(x, g, w):
    t, d = x.shape
    n = w.shape[1]
    tm = min(PROJ_TM, t)
    tn = PROJ_TN
    return pl.pallas_call(
        _norm_matmul_kernel,
        grid=(t // tm, n // tn),
        in_specs=[
            pl.BlockSpec((tm, d), lambda i, j: (i, 0)),
            pl.BlockSpec((1, d), lambda i, j: (0, 0)),
            pl.BlockSpec((d, tn), lambda i, j: (0, j)),
        ],
        out_specs=pl.BlockSpec((tm, tn), lambda i, j: (i, j)),
        out_shape=jax.ShapeDtypeStruct((t, n), F32),
        scratch_shapes=[pltpu.VMEM((tm, d), BF16)],
        compiler_params=_cparams(("parallel", "arbitrary")),
        name="norm_in_proj",
    )(x, g, w)


def _lru_kernel(x_ref, gate_ref, cw_ref, cb_ref, wa_ref, ba_ref, wx_ref, bx_ref, lam_ref,
                ng_ref, o_ref, xbuf, a_s, b_s, hcar):
    c = pl.program_id(1)
    ts = x_ref.shape[1]
    hist = SUBLANES

    @pl.when(c == 0)
    def _():
        xbuf[...] = jnp.zeros((hist, D_RNN), F32)
        hcar[...] = jnp.zeros((1, D_RNN), F32)

    x = x_ref[0]
    prev = xbuf[...]
    rows8 = lax.broadcasted_iota(jnp.int32, (SUBLANES, D_RNN), 0)
    xc = cb_ref[...]
    for tap in range(CONV_WIDTH):
        back = CONV_WIDTH - 1 - tap
        if back == 0:
            xs = x
        else:
            rolled = pltpu.roll(x, back, axis=0)
            head = jnp.where(rows8 < back, pltpu.roll(prev, back, axis=0), rolled[0:hist, :])
            xs = jnp.concatenate([head, rolled[hist:, :]], axis=0)
        xc = xc + xs * cw_ref[tap:tap + 1, :]
    xbuf[...] = x[ts - hist:ts, :]

    xcb = xc.astype(BF16)
    ra = []
    rx = []
    for n in range(LRU_BLOCKS):
        xs = xcb[:, n * LRU_BLOCK:(n + 1) * LRU_BLOCK]
        ra.append(jnp.dot(xs, wa_ref[n], preferred_element_type=F32))
        rx.append(jnp.dot(xs, wx_ref[n], preferred_element_type=F32))
    r = jax.nn.sigmoid(jnp.concatenate(ra, axis=1) + ba_ref[...])
    ig = jax.nn.sigmoid(jnp.concatenate(rx, axis=1) + bx_ref[...])

    z = -lam_ref[...]
    softplus = jnp.maximum(z, 0.0) + jnp.log1p(jnp.exp(-jnp.abs(z)))
    a = jnp.exp2(r * ((-LRU_C * LOG2_E) * softplus))
    om = 1.0 - a * a
    mult = jnp.where(om > 0.0, om * lax.rsqrt(om), 0.0)
    a_s[...] = a
    b_s[...] = mult * ig * xc

    @pl.when(c == 0)
    def _():
        b_s[0:1, :] = ig[0:1, :] * xc[0:1, :]

    def body(i, h):
        r0 = pl.multiple_of(i * SUBLANES, SUBLANES)
        av = a_s[pl.ds(r0, SUBLANES), :]
        bv = b_s[pl.ds(r0, SUBLANES), :]
        for sh in (1, 2, 4):
            a_sh = pltpu.roll(av, sh, axis=0)
            b_sh = pltpu.roll(bv, sh, axis=0)
            m = rows8 >= sh
            bv = jnp.where(m, av * b_sh + bv, bv)
            av = jnp.where(m, av * a_sh, av)
        hv = av * h + bv
        b_s[pl.ds(r0, SUBLANES), :] = hv
        return hv[SUBLANES - 1:SUBLANES, :]

    hcar[...] = lax.fori_loop(0, ts // SUBLANES, body, hcar[...])

    y = _rms(b_s[...], ng_ref[...])
    o_ref[0] = (y * jax.nn.gelu(gate_ref[0])).astype(BF16)


def _lru_branch(proj3, cw, cb, wa, ba, wx, bx, lam, ng, casts):
    b, s, _ = proj3.shape
    ts = min(SEQ_TS, s)
    vec = pl.BlockSpec((1, D_RNN), lambda i, j: (0, 0))
    blk = pl.BlockSpec((LRU_BLOCKS, LRU_BLOCK, LRU_BLOCK), lambda i, j: (0, 0, 0))
    in_specs = [
        pl.BlockSpec((1, ts, D_RNN), lambda i, j: (i, j, 0)),
        pl.BlockSpec((1, ts, D_RNN), lambda i, j: (i, j, 1)),
        pl.BlockSpec((CONV_WIDTH, D_RNN), lambda i, j: (0, 0)),
        vec, blk, vec, blk, vec, vec, vec,
    ]
    outs = pl.pallas_call(
        _with_casts(_lru_kernel, len(in_specs), 1, len(casts)),
        grid=(b, s // ts),
        in_specs=in_specs + [c.in_spec for c in casts],
        out_specs=[pl.BlockSpec((1, ts, D_RNN), lambda i, j: (i, j, 0))]
        + [c.out_spec for c in casts],
        out_shape=[jax.ShapeDtypeStruct((b, s, D_RNN), BF16)] + [c.out_shape for c in casts],
        scratch_shapes=[
            pltpu.VMEM((SUBLANES, D_RNN), F32),
            pltpu.VMEM((ts, D_RNN), F32),
            pltpu.VMEM((ts, D_RNN), F32),
            pltpu.VMEM((1, D_RNN), F32),
        ],
        compiler_params=_cparams(("arbitrary", "arbitrary")),
        name="rglru_branch",
    )(proj3, proj3, cw, cb, wa, ba, wx, bx, lam, ng, *[c.array for c in casts])
    return outs[0], outs[1:]


def _rope(x, cos, sin_signed):
    half = RET_DK // 2
    lane = lax.broadcasted_iota(jnp.int32, x.shape, 1)
    first = (lane & (RET_DK - 1)) < half
    rot = jnp.where(first, pltpu.roll(x, LANES - half, axis=1), pltpu.roll(x, half, axis=1))
    return x * cos + rot * sin_signed


def _ret_kernel(q_ref, k_ref, v_ref, g_ref, cos_ref, sin_ref, xim_ref, zmt_ref, dec_ref,
                cd_ref, gn_ref, o_ref, state):
    c = pl.program_id(1)
    ts = q_ref.shape[1]

    @pl.when(c == 0)
    def _():
        state[...] = jnp.zeros(state.shape, F32)

    lane = lax.broadcasted_iota(jnp.int32, (RET_CHUNK, LANES), 1)
    even_head = lane < RET_DK
    for p in range(RET_HEADS // 2):
        ps = slice(p * LANES, (p + 1) * LANES)
        for n in range(ts // RET_CHUNK):
            rows = slice(n * RET_CHUNK, (n + 1) * RET_CHUNK)
            cos = cos_ref[rows, :]
            sin = sin_ref[rows, :]
            qp = _rope(q_ref[0, rows, ps], cos, sin)
            kt = _rope(k_ref[0, rows, ps], cos, sin).T
            ktb = kt.astype(BF16)
            qm = jnp.concatenate([jnp.where(even_head, qp, 0.0), jnp.where(even_head, 0.0, qp)],
                                 axis=0).astype(BF16)
            sc = jnp.dot(qm, ktb, preferred_element_type=F32)
            for par in range(2):
                h = 2 * p + par
                vs = slice(h * RET_DV, (h + 1) * RET_DV)
                vh = v_ref[0, rows, vs].astype(BF16)
                scores = sc[par * RET_CHUNK:(par + 1) * RET_CHUNK, :] * dec_ref[h]
                inner = jnp.dot(scores.astype(BF16), vh, preferred_element_type=F32)
                st = state[h]
                cross = jnp.dot((qp * xim_ref[h]).astype(BF16), st.astype(BF16),
                                preferred_element_type=F32)
                kv = jnp.dot((kt * zmt_ref[h]).astype(BF16), vh, preferred_element_type=F32)
                state[h] = cd_ref[h] * st + kv
                o = inner + cross
                mu = jnp.mean(o, axis=-1, keepdims=True)
                d = o - mu
                var = jnp.mean(d * d, axis=-1, keepdims=True)
                y = d * lax.rsqrt(var + EPS) * gn_ref[:, vs]
                o_ref[0, rows, vs] = (y * jax.nn.silu(g_ref[0, rows, vs])).astype(BF16)


def _ret_tables(s):
    scale = RET_DK ** -0.5
    inv_freq = ROPE_BASE ** (-jnp.arange(0, RET_DK, 2, dtype=F32) / RET_DK)
    ang = jnp.arange(s, dtype=F32)[:, None] * inv_freq[None, :]
    cos = jnp.cos(ang)
    sin = jnp.sin(ang)
    cos_t = jnp.tile(cos, (1, 2 * LANES // RET_DK))
    sin_t = jnp.tile(jnp.concatenate([-sin, sin], axis=1), (1, LANES // RET_DK))
    log_g = jnp.log(1.0 - 2.0 ** (-5.0 - jnp.arange(RET_HEADS, dtype=F32)))
    idx = jnp.arange(RET_CHUNK, dtype=F32)
    rel = idx[:, None] - idx[None, :]
    decay = jnp.where(rel >= 0, jnp.exp(jnp.maximum(rel, 0.0)[None] * log_g[:, None, None]), 0.0)
    zeta = jnp.exp((RET_CHUNK - 1.0 - idx)[None, :] * log_g[:, None])
    xi = jnp.exp((idx + 1.0)[None, :] * log_g[:, None])
    own = (jnp.arange(LANES)[None, :] // RET_DK) == (jnp.arange(RET_HEADS)[:, None] % 2)
    xim = jnp.where(own[:, None, :], xi[:, :, None], 0.0)
    zmt = jnp.where(own[:, :, None], scale * zeta[:, None, :], 0.0)
    cdec = jnp.exp(RET_CHUNK * log_g)
    return cos_t, sin_t, xim, zmt, decay * scale, cdec


def _ret_branch(proj3, tables, gn):
    b, s, _ = proj3.shape
    ts = min(SEQ_TS, s)
    cos_t, sin_t, xim, zmt, decay, cdec = tables
    q_blk = 2 * D_RNN // D_QK
    per_head = pl.BlockSpec((RET_HEADS, RET_CHUNK, RET_CHUNK), lambda i, j: (0, 0, 0))
    assert RET_CHUNK == LANES == RET_DV == 2 * RET_DK
    in_specs = [
        pl.BlockSpec((1, ts, D_QK), lambda i, j: (i, j, q_blk)),
        pl.BlockSpec((1, ts, D_QK), lambda i, j: (i, j, q_blk + 1)),
        pl.BlockSpec((1, ts, D_RET), lambda i, j: (i, j, 3)),
        pl.BlockSpec((1, ts, D_RET), lambda i, j: (i, j, 4)),
        pl.BlockSpec((ts, LANES), lambda i, j: (j, 0)),
        pl.BlockSpec((ts, LANES), lambda i, j: (j, 0)),
        per_head, per_head, per_head,
        pl.BlockSpec(memory_space=pltpu.SMEM),
        pl.BlockSpec((1, D_RET), lambda i, j: (0, 0)),
    ]
    return pl.pallas_call(
        _ret_kernel,
        grid=(b, s // ts),
        in_specs=in_specs,
        out_specs=pl.BlockSpec((1, ts, D_RET), lambda i, j: (i, j, 0)),
        out_shape=jax.ShapeDtypeStruct((b, s, D_RET), BF16),
        scratch_shapes=[pltpu.VMEM((RET_HEADS, LANES, RET_DV), F32)],
        compiler_params=_cparams(("parallel", "arbitrary")),
        name="retention_branch",
    )(proj3, proj3, proj3, proj3, cos_t, sin_t, xim, zmt, decay, cdec, gn)


def _out_proj_kernel(yl_ref, yr_ref, x_ref, w_ref, g_ref, x1_ref, h_ref):
    acc = jnp.dot(yl_ref[...], w_ref[0:D_RNN, :], preferred_element_type=F32)
    acc = acc + jnp.dot(yr_ref[...], w_ref[D_RNN:D_MODEL, :], preferred_element_type=F32)
    x1 = x_ref[...] + acc
    x1_ref[...] = x1
    h_ref[...] = _rms(x1, g_ref[...]).astype(BF16)


def _out_proj_router_kernel(yl_ref, yr_ref, x_ref, w_ref, g_ref, rw_ref, x1_ref, hp_ref, idx_ref,
                            gate_ref):
    acc = jnp.dot(yl_ref[...], w_ref[0:D_RNN, :], preferred_element_type=F32)
    acc = acc + jnp.dot(yr_ref[...], w_ref[D_RNN:D_MODEL, :], preferred_element_type=F32)
    x1 = x_ref[...] + acc
    x1_ref[...] = x1
    h = _rms(x1, g_ref[...]).astype(BF16)
    hf = h.astype(F32)
    for a in range(hp_ref.shape[1]):
        hp_ref[:, a, :] = hf[:, a * LANES:(a + 1) * LANES]
    logits = jnp.dot(h, rw_ref[...], preferred_element_type=F32)
    lane_i = lax.broadcasted_iota(jnp.int32, logits.shape, 1)
    lane = lane_i.astype(F32)
    neg = jnp.float32(-jnp.inf)
    logits = jnp.where(lane_i < N_EXPERTS, logits, neg)
    m1 = jnp.max(logits, axis=-1, keepdims=True)
    i1 = jnp.min(jnp.where(logits == m1, lane, float(LANES)), axis=-1, keepdims=True)
    rest = jnp.where(lane == i1, neg, logits)
    m2 = jnp.max(rest, axis=-1, keepdims=True)
    i2 = jnp.min(jnp.where(rest == m2, lane, float(LANES)), axis=-1, keepdims=True)
    e2 = jnp.exp(m2 - m1)
    den = 1.0 + e2
    idx_ref[...] = jnp.where(lane_i == 0, i1, i2).astype(jnp.int32)
    gate_ref[...] = jnp.where(lane_i == 0, 1.0 / den, e2 / den)


def _out_proj(yl, yr, x, w, g, router=None):
    t, d = x.shape
    tm = min(OUT_TM, t)
    row = lambda i: (i, 0)
    fixed = lambda i: (0, 0)
    in_specs = [
        pl.BlockSpec((tm, D_RNN), row),
        pl.BlockSpec((tm, D_RET), row),
        pl.BlockSpec((tm, d), row),
        pl.BlockSpec((d, d), fixed),
        pl.BlockSpec((1, d), fixed),
    ]
    if router is None:
        return pl.pallas_call(
            _out_proj_kernel,
            grid=(t // tm,),
            in_specs=in_specs,
            out_specs=[pl.BlockSpec((tm, d), row), pl.BlockSpec((tm, d), row)],
            out_shape=[jax.ShapeDtypeStruct((t, d), F32), jax.ShapeDtypeStruct((t, d), BF16)],
            compiler_params=_cparams(("parallel",)),
            name="out_proj_norm",
        )(yl, yr, x, w, g)
    return pl.pallas_call(
        _out_proj_router_kernel,
        grid=(t // tm,),
        in_specs=in_specs + [pl.BlockSpec((d, LANES), fixed)],
        out_specs=[pl.BlockSpec((tm, d), row),
                   pl.BlockSpec((tm, d // LANES, LANES), lambda i: (i, 0, 0)),
                   pl.BlockSpec((tm, LANES), row), pl.BlockSpec((tm, LANES), row)],
        out_shape=[jax.ShapeDtypeStruct((t, d), F32),
                   jax.ShapeDtypeStruct((t, d // LANES, LANES), F32),
                   jax.ShapeDtypeStruct((t, LANES), jnp.int32),
                   jax.ShapeDtypeStruct((t, LANES), F32)],
        compiler_params=_cparams(("parallel",)),
        name="out_proj_router",
    )(yl, yr, x, w, g, router)


def _swiglu_step(h, wg, wu, wd):
    gt = jnp.dot(h, wg, preferred_element_type=F32)
    up = jnp.dot(h, wu, preferred_element_type=F32)
    act = (jax.nn.silu(gt) * up).astype(BF16)
    return jnp.dot(act, wd, preferred_element_type=F32)


def _ffn_kernel(h_ref, x_ref, wg_ref, wu_ref, wd_ref, o_ref):
    @pl.when(pl.program_id(1) == 0)
    def _():
        o_ref[...] = x_ref[...]

    o_ref[...] += _swiglu_step(h_ref[...], wg_ref[0], wu_ref[0], wd_ref[...])


def _dense_ffn(h, x, wg, wu, wd, make_casts):
    t, d = x.shape
    tm = min(FFN_TM, t)
    tf = FFN_TF
    grid = (t // tm, D_FF // tf)
    casts = make_casts(*grid)
    in_specs = [
        pl.BlockSpec((tm, d), lambda i, f: (i, 0)),
        pl.BlockSpec((tm, d), lambda i, f: (i, 0)),
        pl.BlockSpec((1, d, tf), lambda i, f: (f, 0, 0)),
        pl.BlockSpec((1, d, tf), lambda i, f: (f, 0, 0)),
        pl.BlockSpec((tf, d), lambda i, f: (f, 0)),
    ]
    outs = pl.pallas_call(
        _with_casts(_ffn_kernel, len(in_specs), 1, len(casts)),
        grid=grid,
        in_specs=in_specs + [c.in_spec for c in casts],
        out_specs=[pl.BlockSpec((tm, d), lambda i, f: (i, 0))] + [c.out_spec for c in casts],
        out_shape=[jax.ShapeDtypeStruct((t, d), F32)] + [c.out_shape for c in casts],
        compiler_params=_cparams(("arbitrary", "arbitrary")),
        name="dense_swiglu",
    )(h, x, wg, wu, wd, *[c.array for c in casts])
    return outs[0], outs[1:]


def _moe_kernel(te_ref, tv_ref, src_ref, nxt_ref, hp_hbm, wg_ref, wu_ref, wd_ref, o_ref,
                xg, wide, hs, sem):
    i = pl.program_id(0)
    f = pl.program_id(1)
    rows = xg.shape[1]
    tm = hs.shape[0]
    per_step = rows // (D_FF // FFN_TF)
    slot = i % 2
    valid = tv_ref[i] == 1

    def start_row(idx_ref, dst_slot, r):
        tok = idx_ref[0, 0, r]
        pltpu.make_async_copy(hp_hbm.at[tok], xg.at[dst_slot, r], sem.at[dst_slot]).start()

    @pl.when((i == 0) & (f == 0))
    def _():
        def body(r, carry):
            start_row(src_ref, 0, r)
            return carry

        lax.fori_loop(0, rows, body, 0, unroll=8)

    @pl.when((f == 0) & ((i == 0) | (tv_ref[jnp.maximum(i - 1, 0)] == 1)))
    def _():
        pltpu.make_async_copy(xg.at[1 - slot], xg.at[slot], sem.at[slot]).wait()

    @pl.when(f == 0)
    def _():
        o_ref[...] = jnp.zeros(o_ref.shape, F32)

    @pl.when(valid & (f == 0))
    def _():
        for a in range(xg.shape[2]):
            wide[:, a * LANES:(a + 1) * LANES] = xg[slot, 0:tm, a, :]
        hs[...] = wide[...].astype(BF16)

    @pl.when(valid)
    def _():
        h = hs[...]
        gt = jnp.dot(h, wg_ref[0, 0], preferred_element_type=F32)
        up = jnp.dot(h, wu_ref[0, 0], preferred_element_type=F32)
        for k in range(per_step):
            start_row(nxt_ref, 1 - slot, f * per_step + k)
        act = (jax.nn.silu(gt) * up).astype(BF16)
        o_ref[...] += jnp.dot(act, wd_ref[0], preferred_element_type=F32)


def _gather_rows(tm):
    nf = D_FF // FFN_TF
    return nf * pl.cdiv(tm, nf)


def _moe_ffn(hp, tile_expert, tile_valid, src3, wg, wu, wd):
    t = hp.shape[0]
    d = wd.shape[2]
    n_tiles, _, width = src3.shape
    tm = min(FFN_TM, t)
    tf = FFN_TF
    nf = D_FF // tf

    def w_col(i, f, te, tv):
        return (te[i], jnp.where(tv[i] == 1, f, nf - 1), 0, 0)

    def w_row(i, f, te, tv):
        return (te[i], jnp.where(tv[i] == 1, f, nf - 1), 0)

    grid_spec = pltpu.PrefetchScalarGridSpec(
        num_scalar_prefetch=2,
        grid=(n_tiles, nf),
        in_specs=[
            pl.BlockSpec((1, 1, width), lambda i, f, te, tv: (i, 0, 0), memory_space=pltpu.SMEM),
            pl.BlockSpec((1, 1, width),
                         lambda i, f, te, tv: (jnp.minimum(i + 1, n_tiles - 1), 0, 0),
                         memory_space=pltpu.SMEM),
            pl.BlockSpec(memory_space=pl.ANY),
            pl.BlockSpec((1, 1, d, tf), w_col),
            pl.BlockSpec((1, 1, d, tf), w_col),
            pl.BlockSpec((1, tf, d), w_row),
        ],
        out_specs=pl.BlockSpec((tm, d), lambda i, f, te, tv: (i, 0)),
        scratch_shapes=[
            pltpu.VMEM((2, _gather_rows(tm), d // LANES, LANES), F32),
            pltpu.VMEM((tm, d), F32),
            pltpu.VMEM((tm, d), BF16),
            pltpu.SemaphoreType.DMA((2,)),
        ],
    )
    return pl.pallas_call(
        _moe_kernel,
        grid_spec=grid_spec,
        out_shape=jax.ShapeDtypeStruct((n_tiles * tm, d), F32),
        compiler_params=_cparams(("arbitrary", "arbitrary")),
        name="expert_swiglu",
    )(tile_expert, tile_valid, src3, src3, hp, wg, wu, wd)


def _combine_kernel(pos_ref, nxt_ref, x_ref, gate_ref, g_ref, ys_hbm, o_ref, buf, sem):
    i = pl.program_id(0)
    tt = x_ref.shape[0]
    slot = i % 2

    def start_rows(idx_ref, dst_slot):
        def body(grp, carry):
            for sub in range(SUBLANES):
                for k in range(TOP_K):
                    p = idx_ref[0, 0, TOP_K * (grp * SUBLANES + sub) + k]
                    pltpu.make_async_copy(ys_hbm.at[pl.ds(p, 1)],
                                          buf.at[dst_slot, k, grp, pl.ds(sub, 1)],
                                          sem.at[dst_slot]).start()
            return carry

        lax.fori_loop(0, tt // SUBLANES, body, 0)

    @pl.when(i == 0)
    def _():
        start_rows(pos_ref, 0)

    pltpu.make_async_copy(buf.at[1 - slot], buf.at[slot], sem.at[slot]).wait()

    @pl.when(i + 1 < pl.num_programs(0))
    def _():
        start_rows(nxt_ref, 1 - slot)

    d = x_ref.shape[1]
    gates = gate_ref[...]
    moe = gates[:, 0:1] * buf[slot, 0].reshape(tt, d)
    for k in range(1, TOP_K):
        moe = moe + gates[:, k:k + 1] * buf[slot, k].reshape(tt, d)
    o_ref[...] = _rms(x_ref[...] + moe, g_ref[...])


def _combine(x, gates, pos3, ys, g):
    t, d = x.shape
    n_steps, _, per = pos3.shape
    tt = per // TOP_K
    return pl.pallas_call(
        _combine_kernel,
        grid=(n_steps,),
        in_specs=[
            pl.BlockSpec((1, 1, per), lambda i: (i, 0, 0), memory_space=pltpu.SMEM),
            pl.BlockSpec((1, 1, per), lambda i: (jnp.minimum(i + 1, n_steps - 1), 0, 0),
                         memory_space=pltpu.SMEM),
            pl.BlockSpec((tt, d), lambda i: (i, 0)),
            pl.BlockSpec((tt, LANES), lambda i: (i, 0)),
            pl.BlockSpec((1, d), lambda i: (0, 0)),
            pl.BlockSpec(memory_space=pl.ANY),
        ],
        out_specs=pl.BlockSpec((tt, d), lambda i: (i, 0)),
        out_shape=jax.ShapeDtypeStruct((t, d), F32),
        scratch_shapes=[pltpu.VMEM((2, TOP_K, tt // SUBLANES, SUBLANES, d), F32),
                        pltpu.SemaphoreType.DMA((2,))],
        compiler_params=_cparams(("arbitrary",)),
        name="combine_final_norm",
    )(pos3, pos3, x, gates, g, ys)


def _routing_plan(top_idx, tm):
    t = top_idx.shape[0]
    n_slots = t * TOP_K
    e_flat = top_idx.reshape(n_slots)
    onehot = (e_flat[:, None] == jnp.arange(N_EXPERTS, dtype=jnp.int32)[None, :]).astype(jnp.int32)
    csum = jnp.cumsum(onehot, axis=0)
    counts = csum[-1]
    rank = jnp.sum((csum - onehot) * onehot, axis=1)
    padded = ((counts + tm - 1) // tm) * tm
    ends = jnp.cumsum(padded)
    starts = ends - padded
    pos = jnp.sum(starts[None, :] * onehot, axis=1) + rank
    n_tiles = n_slots // tm + N_EXPERTS + 1
    src = jnp.zeros((n_tiles * tm,), jnp.int32).at[pos].set(
        jnp.arange(n_slots, dtype=jnp.int32) // TOP_K)
    width = LANES * pl.cdiv(_gather_rows(tm), LANES)
    src = jnp.pad(src.reshape(n_tiles, tm), ((0, 0), (0, width - tm)))
    tile_start = jnp.arange(n_tiles, dtype=jnp.int32) * tm
    tile_expert = jnp.minimum(
        jnp.sum((tile_start[:, None] >= ends[None, :]).astype(jnp.int32), axis=1), N_EXPERTS - 1)
    tile_valid = (tile_start < ends[-1]).astype(jnp.int32)
    return pos, src.reshape(n_tiles, 1, width), tile_expert, tile_valid


def kernel(x, norm1_g, w_in, conv_w, conv_b, lru_wa, lru_ba, lru_wx, lru_bx, lru_lambda,
           lru_norm_g, ret_norm_g, w_out, norm2_g, ffn_w_gate, ffn_w_up, ffn_w_down,
           moe_router, moe_w_gate, moe_w_up, moe_w_down, final_norm_g):
    assert DEPTH == 2 and w_in.shape[0] == DEPTH
    b, s, d = x.shape
    t = b * s
    tables = _ret_tables(s)
    xf = x.reshape(t, d)
    row = lambda v: v.reshape(1, -1)

    def branches(layer, x_in, w_in_bf16, lru_casts):
        proj = _norm_matmul(x_in, row(norm1_g[layer]), w_in_bf16)
        proj3 = proj.reshape(b, s, D_IN)
        y_lru, lru_cast = _lru_branch(proj3, conv_w[layer], row(conv_b[layer]),
                                      lru_wa[layer].astype(BF16), row(lru_ba[layer]),
                                      lru_wx[layer].astype(BF16), row(lru_bx[layer]),
                                      row(lru_lambda[layer]), row(lru_norm_g[layer]), lru_casts)
        y_ret = _ret_branch(proj3, tables, row(ret_norm_g[layer]))
        return y_lru.reshape(t, D_RNN), y_ret.reshape(t, D_RET), lru_cast

    flat2 = lambda w: w.reshape(-1, w.shape[-1])
    inner = s // min(SEQ_TS, s)
    n_steps = b * inner
    step = lambda i, j: i * inner + j
    yl, yr, (wg0, wu0, wd0, wo0, wi1, wo1) = branches(
        0, xf, w_in[0].astype(BF16),
        [_tiled_cast(ffn_w_gate[0], 1, n_steps, step), _tiled_cast(ffn_w_up[0], 1, n_steps, step),
         _flat_cast(ffn_w_down[0], n_steps, step),
         _flat_cast(flat2(w_out), n_steps, step, 0, DEPTH),
         _flat_cast(flat2(w_in), n_steps, step, 1, DEPTH),
         _flat_cast(flat2(w_out), n_steps, step, 1, DEPTH)])
    x1, h2 = _out_proj(yl, yr, xf, wo0, row(norm2_g[0]))
    xf, (moe_g, moe_d) = _dense_ffn(
        h2, x1, wg0[0], wu0[0], wd0,
        lambda ni, nf: [_grid_tiled_cast(flat2(moe_w_gate[0]), N_EXPERTS, ni),
                        _flat_cast(flat2(moe_w_down[0]), ni * nf, lambda i, f: i * nf + f)])
    yl, yr, (moe_u,) = branches(
        1, xf, wi1, [_tiled_cast(flat2(moe_w_up[0]), N_EXPERTS, n_steps, step)])
    router = jnp.pad(moe_router[0], ((0, 0), (0, LANES - N_EXPERTS))).astype(BF16)
    x1, hp, idx, gates = _out_proj(yl, yr, xf, wo1, row(norm2_g[1]), router)
    tm = min(FFN_TM, t)
    pos, src3, tile_expert, tile_valid = _routing_plan(idx[:, :TOP_K], tm)
    ys = _moe_ffn(hp, tile_expert, tile_valid, src3,
                  moe_g, moe_u, moe_d.reshape(moe_w_down.shape[1:]))
    tt = min(COMB_TT, t)
    pos3 = pos.reshape(t // tt, 1, tt * TOP_K)
    out = _combine(x1, gates, pos3, ys, row(final_norm_g))
    return out.reshape(b, s, d)
```

```python
import functools

import jax
import jax.numpy as jnp
from jax import lax
from jax.experimental import pallas as pl
from jax.experimental.pallas import tpu as pltpu

F32 = jnp.float32
BF16 = jnp.bfloat16

D_MODEL = 2048
DEPTH = 2
D_RNN = D_MODEL // 2
D_RET = D_MODEL - D_RNN
LRU_BLOCKS = 8
LRU_BLOCK = D_RNN // LRU_BLOCKS
CONV_WIDTH = 4
LRU_C = 8.0
RET_HEADS = 8
RET_DV = D_RET // RET_HEADS
RET_DK = RET_DV // 2
RET_CHUNK = 128
ROPE_BASE = 10000.0
D_FF = 256 * ((8 * D_MODEL // 3 + 255) // 256)
N_EXPERTS = 8
TOP_K = 2
EPS = 1e-6
LOG2_E = 1.4426950408889634
D_QK = RET_HEADS * RET_DK
D_IN = 2 * D_RNN + 2 * D_QK + 2 * D_RET

LANES = 128
SUBLANES = 8
VMEM_LIMIT_BYTES = 56 * 1024 * 1024

PROJ_TM = 1024
PROJ_TN = 1280
SEQ_TS = 256
OUT_TM = 512
FFN_TM = 512
FFN_TF = 512
COMB_TT = 256


def _cparams(sem):
    return pltpu.CompilerParams(dimension_semantics=sem, vmem_limit_bytes=VMEM_LIMIT_BYTES)


def _rms(x, g):
    return x * lax.rsqrt(jnp.mean(x * x, axis=-1, keepdims=True) + EPS) * g


def _with_casts(body, n_in, n_out, n_cast):
    def kern(*refs):
        ins = refs[:n_in]
        cast_in = refs[n_in:n_in + n_cast]
        outs = refs[n_in + n_cast:n_in + n_cast + n_out]
        cast_out = refs[n_in + n_cast + n_out:n_in + 2 * n_cast + n_out]
        scratch = refs[n_in + 2 * n_cast + n_out:]
        for src, dst in zip(cast_in, cast_out):
            _cast_block(src, dst)
        body(*ins, *outs, *scratch)

    return kern


def _cast_block(src, dst):
    if len(dst.shape) == 2:
        dst[...] = src[...].astype(BF16)
        return
    n, _, tf = dst.shape[-3:]
    lead = (0,) * (len(dst.shape) - 3)
    for f in range(n):
        dst[lead + (f,)] = src[:, f * tf:(f + 1) * tf].astype(BF16)


class _Cast:
    def __init__(self, array, in_spec, out_spec, out_shape):
        self.array, self.in_spec, self.out_spec = array, in_spec, out_spec
        self.out_shape = jax.ShapeDtypeStruct(out_shape, BF16)


def _flat_cast(a, n_steps, step_of, part=0, n_parts=1):
    rows_all, cols = a.shape
    rows = rows_all // n_parts
    blk = rows // n_steps
    in_spec = pl.BlockSpec((blk, cols), lambda *ids: (part * n_steps + step_of(*ids), 0))
    out_spec = pl.BlockSpec((blk, cols), lambda *ids: (step_of(*ids), 0))
    return _Cast(a, in_spec, out_spec, (rows, cols))


def _tiled_cast(a, n_experts, n_steps, step_of):
    rows_all, cols = a.shape
    rows = rows_all // n_experts
    blk = rows_all // n_steps
    per = rows // blk
    in_spec = pl.BlockSpec((blk, cols), lambda *ids: (step_of(*ids), 0))
    out_spec = pl.BlockSpec((1, cols // FFN_TF, blk, FFN_TF),
                            lambda *ids: (step_of(*ids) // per, 0, step_of(*ids) % per, 0))
    return _Cast(a, in_spec, out_spec, (n_experts, cols // FFN_TF, rows, FFN_TF))


def _grid_tiled_cast(a, n_experts, n_row_steps):
    rows_all, cols = a.shape
    rows = rows_all // n_experts
    blk = rows_all // n_row_steps
    per = rows // blk
    in_spec = pl.BlockSpec((blk, FFN_TF), lambda i, f: (i, f))
    out_spec = pl.BlockSpec((1, 1, blk, FFN_TF), lambda i, f: (i // per, f, i % per, 0))
    return _Cast(a, in_spec, out_spec, (n_experts, cols // FFN_TF, rows, FFN_TF))


def _norm_matmul_kernel(x_ref, g_ref, w_ref, o_ref, h_ref):
    @pl.when(pl.program_id(1) == 0)
    def _():
        h_ref[...] = _rms(x_ref[...], g_ref[...]).astype(BF16)

    o_ref[...] = jnp.dot(h_ref[...], w_ref[...], preferred_element_type=F32)


def _norm_matmul(x, g, w):
    t, d = x.shape
    n = w.shape[1]
    tm = min(PROJ_TM, t)
    tn = PROJ_TN
    return pl.pallas_call(
        _norm_matmul_kernel,
        grid=(t // tm, n // tn),
        in_specs=[
            pl.BlockSpec((tm, d), lambda i, j: (i, 0)),
            pl.BlockSpec((1, d), lambda i, j: (0, 0)),
            pl.BlockSpec((d, tn), lambda i, j: (0, j)),
        ],
        out_specs=pl.BlockSpec((tm, tn), lambda i, j: (i, j)),
        out_shape=jax.ShapeDtypeStruct((t, n), F32),
        scratch_shapes=[pltpu.VMEM((tm, d), BF16)],
        compiler_params=_cparams(("parallel", "arbitrary")),
        name="norm_in_proj",
    )(x, g, w)


def _lru_kernel(x_ref, gate_ref, cw_ref, cb_ref, wa_ref, ba_ref, wx_ref, bx_ref, lam_ref,
                ng_ref, o_ref, xbuf, a_s, b_s, hcar):
    c = pl.program_id(1)
    ts = x_ref.shape[1]
    hist = SUBLANES

    @pl.when(c == 0)
    def _():
        xbuf[...] = jnp.zeros((hist, D_RNN), F32)
        hcar[...] = jnp.zeros((1, D_RNN), F32)

    x = x_ref[0]
    prev = xbuf[...]
    rows8 = lax.broadcasted_iota(jnp.int32, (SUBLANES, D_RNN), 0)
    xc = cb_ref[...]
    for tap in range(CONV_WIDTH):
        back = CONV_WIDTH - 1 - tap
        if back == 0:
            xs = x
        else:
            rolled = pltpu.roll(x, back, axis=0)
            head = jnp.where(rows8 < back, pltpu.roll(prev, back, axis=0), rolled[0:hist, :])
            xs = jnp.concatenate([head, rolled[hist:, :]], axis=0)
        xc = xc + xs * cw_ref[tap:tap + 1, :]
    xbuf[...] = x[ts - hist:ts, :]

    xcb = xc.astype(BF16)
    ra = []
    rx = []
    for n in range(LRU_BLOCKS):
        xs = xcb[:, n * LRU_BLOCK:(n + 1) * LRU_BLOCK]
        ra.append(jnp.dot(xs, wa_ref[n], preferred_element_type=F32))
        rx.append(jnp.dot(xs, wx_ref[n], preferred_element_type=F32))
    r = jax.nn.sigmoid(jnp.concatenate(ra, axis=1) + ba_ref[...])
    ig = jax.nn.sigmoid(jnp.concatenate(rx, axis=1) + bx_ref[...])

    z = -lam_ref[...]
    softplus = jnp.maximum(z, 0.0) + jnp.log1p(jnp.exp(-jnp.abs(z)))
    a = jnp.exp2(r * ((-LRU_C * LOG2_E) * softplus))
    om = 1.0 - a * a
    mult = jnp.where(om > 0.0, om * lax.rsqrt(om), 0.0)
    a_s[...] = a
    b_s[...] = mult * ig * xc

    @pl.when(c == 0)
    def _():
        b_s[0:1, :] = ig[0:1, :] * xc[0:1, :]

    def body(i, h):
        r0 = pl.multiple_of(i * SUBLANES, SUBLANES)
        av = a_s[pl.ds(r0, SUBLANES), :]
        bv = b_s[pl.ds(r0, SUBLANES), :]
        for sh in (1, 2, 4):
            a_sh = pltpu.roll(av, sh, axis=0)
            b_sh = pltpu.roll(bv, sh, axis=0)
            m = rows8 >= sh
            bv = jnp.where(m, av * b_sh + bv, bv)
            av = jnp.where(m, av * a_sh, av)
        hv = av * h + bv
        b_s[pl.ds(r0, SUBLANES), :] = hv
        return hv[SUBLANES - 1:SUBLANES, :]

    hcar[...] = lax.fori_loop(0, ts // SUBLANES, body, hcar[...])

    y = _rms(b_s[...], ng_ref[...])
    o_ref[0] = (y * jax.nn.gelu(gate_ref[0])).astype(BF16)


def _lru_branch(proj3, cw, cb, wa, ba, wx, bx, lam, ng, casts):
    b, s, _ = proj3.shape
    ts = min(SEQ_TS, s)
    vec = pl.BlockSpec((1, D_RNN), lambda i, j: (0, 0))
    blk = pl.BlockSpec((LRU_BLOCKS, LRU_BLOCK, LRU_BLOCK), lambda i, j: (0, 0, 0))
    in_specs = [
        pl.BlockSpec((1, ts, D_RNN), lambda i, j: (i, j, 0)),
        pl.BlockSpec((1, ts, D_RNN), lambda i, j: (i, j, 1)),
        pl.BlockSpec((CONV_WIDTH, D_RNN), lambda i, j: (0, 0)),
        vec, blk, vec, blk, vec, vec, vec,
    ]
    outs = pl.pallas_call(
        _with_casts(_lru_kernel, len(in_specs), 1, len(casts)),
        grid=(b, s // ts),
        in_specs=in_specs + [c.in_spec for c in casts],
        out_specs=[pl.BlockSpec((1, ts, D_RNN), lambda i, j: (i, j, 0))]
        + [c.out_spec for c in casts],
        out_shape=[jax.ShapeDtypeStruct((b, s, D_RNN), BF16)] + [c.out_shape for c in casts],
        scratch_shapes=[
            pltpu.VMEM((SUBLANES, D_RNN), F32),
            pltpu.VMEM((ts, D_RNN), F32),
            pltpu.VMEM((ts, D_RNN), F32),
            pltpu.VMEM((1, D_RNN), F32),
        ],
        compiler_params=_cparams(("arbitrary", "arbitrary")),
        name="rglru_branch",
    )(proj3, proj3, cw, cb, wa, ba, wx, bx, lam, ng, *[c.array for c in casts])
    return outs[0], outs[1:]


def _rope(x, cos, sin_signed):
    half = RET_DK // 2
    lane = lax.broadcasted_iota(jnp.int32, x.shape, 1)
    first = (lane & (RET_DK - 1)) < half
    rot = jnp.where(first, pltpu.roll(x, LANES - half, axis=1), pltpu.roll(x, half, axis=1))
    return x * cos + rot * sin_signed


def _ret_kernel(q_ref, k_ref, v_ref, g_ref, cos_ref, sin_ref, xim_ref, zmt_ref, dec_ref,
                cd_ref, gn_ref, o_ref, state):
    c = pl.program_id(1)
    ts = q_ref.shape[1]

    @pl.when(c == 0)
    def _():
        state[...] = jnp.zeros(state.shape, F32)

    lane = lax.broadcasted_iota(jnp.int32, (RET_CHUNK, LANES), 1)
    even_head = lane < RET_DK
    for p in range(RET_HEADS // 2):
        ps = slice(p * LANES, (p + 1) * LANES)
        for n in range(ts // RET_CHUNK):
            rows = slice(n * RET_CHUNK, (n + 1) * RET_CHUNK)
            cos = cos_ref[rows, :]
            sin = sin_ref[rows, :]
            qp = _rope(q_ref[0, rows, ps], cos, sin)
            kt = _rope(k_ref[0, rows, ps], cos, sin).T
            ktb = kt.astype(BF16)
            qm = jnp.concatenate([jnp.where(even_head, qp, 0.0), jnp.where(even_head, 0.0, qp)],
                                 axis=0).astype(BF16)
            sc = jnp.dot(qm, ktb, preferred_element_type=F32)
            for par in range(2):
                h = 2 * p + par
                vs = slice(h * RET_DV, (h + 1) * RET_DV)
                vh = v_ref[0, rows, vs].astype(BF16)
                scores = sc[par * RET_CHUNK:(par + 1) * RET_CHUNK, :] * dec_ref[h]
                inner = jnp.dot(scores.astype(BF16), vh, preferred_element_type=F32)
                st = state[h]
                cross = jnp.dot((qp * xim_ref[h]).astype(BF16), st.astype(BF16),
                                preferred_element_type=F32)
                kv = jnp.dot((kt * zmt_ref[h]).astype(BF16), vh, preferred_element_type=F32)
                state[h] = cd_ref[h] * st + kv
                o = inner + cross
                mu = jnp.mean(o, axis=-1, keepdims=True)
                d = o - mu
                var = jnp.mean(d * d, axis=-1, keepdims=True)
                y = d * lax.rsqrt(var + EPS) * gn_ref[:, vs]
                o_ref[0, rows, vs] = (y * jax.nn.silu(g_ref[0, rows, vs])).astype(BF16)


def _ret_tables(s):
    scale = RET_DK ** -0.5
    inv_freq = ROPE_BASE ** (-jnp.arange(0, RET_DK, 2, dtype=F32) / RET_DK)
    ang = jnp.arange(s, dtype=F32)[:, None] * inv_freq[None, :]
    cos = jnp.cos(ang)
    sin = jnp.sin(ang)
    cos_t = jnp.tile(cos, (1, 2 * LANES // RET_DK))
    sin_t = jnp.tile(jnp.concatenate([-sin, sin], axis=1), (1, LANES // RET_DK))
    log_g = jnp.log(1.0 - 2.0 ** (-5.0 - jnp.arange(RET_HEADS, dtype=F32)))
    idx = jnp.arange(RET_CHUNK, dtype=F32)
    rel = idx[:, None] - idx[None, :]
    decay = jnp.where(rel >= 0, jnp.exp(jnp.maximum(rel, 0.0)[None] * log_g[:, None, None]), 0.0)
    zeta = jnp.exp((RET_CHUNK - 1.0 - idx)[None, :] * log_g[:, None])
    xi = jnp.exp((idx + 1.0)[None, :] * log_g[:, None])
    own = (jnp.arange(LANES)[None, :] // RET_DK) == (jnp.arange(RET_HEADS)[:, None] % 2)
    xim = jnp.where(own[:, None, :], xi[:, :, None], 0.0)
    zmt = jnp.where(own[:, :, None], scale * zeta[:, None, :], 0.0)
    cdec = jnp.exp(RET_CHUNK * log_g)
    return cos_t, sin_t, xim, zmt, decay * scale, cdec


def _ret_branch(proj3, tables, gn):
    b, s, _ = proj3.shape
    ts = min(SEQ_TS, s)
    cos_t, sin_t, xim, zmt, decay, cdec = tables
    q_blk = 2 * D_RNN // D_QK
    per_head = pl.BlockSpec((RET_HEADS, RET_CHUNK, RET_CHUNK), lambda i, j: (0, 0, 0))
    assert RET_CHUNK == LANES == RET_DV == 2 * RET_DK
    in_specs = [
        pl.BlockSpec((1, ts, D_QK), lambda i, j: (i, j, q_blk)),
        pl.BlockSpec((1, ts, D_QK), lambda i, j: (i, j, q_blk + 1)),
        pl.BlockSpec((1, ts, D_RET), lambda i, j: (i, j, 3)),
        pl.BlockSpec((1, ts, D_RET), lambda i, j: (i, j, 4)),
        pl.BlockSpec((ts, LANES), lambda i, j: (j, 0)),
        pl.BlockSpec((ts, LANES), lambda i, j: (j, 0)),
        per_head, per_head, per_head,
        pl.BlockSpec(memory_space=pltpu.SMEM),
        pl.BlockSpec((1, D_RET), lambda i, j: (0, 0)),
    ]
    return pl.pallas_call(
        _ret_kernel,
        grid=(b, s // ts),
        in_specs=in_specs,
        out_specs=pl.BlockSpec((1, ts, D_RET), lambda i, j: (i, j, 0)),
        out_shape=jax.ShapeDtypeStruct((b, s, D_RET), BF16),
        scratch_shapes=[pltpu.VMEM((RET_HEADS, LANES, RET_DV), F32)],
        compiler_params=_cparams(("parallel", "arbitrary")),
        name="retention_branch",
    )(proj3, proj3, proj3, proj3, cos_t, sin_t, xim, zmt, decay, cdec, gn)


def _out_proj_kernel(yl_ref, yr_ref, x_ref, w_ref, g_ref, x1_ref, h_ref):
    acc = jnp.dot(yl_ref[...], w_ref[0:D_RNN, :], preferred_element_type=F32)
    acc = acc + jnp.dot(yr_ref[...], w_ref[D_RNN:D_MODEL, :], preferred_element_type=F32)
    x1 = x_ref[...] + acc
    x1_ref[...] = x1
    h_ref[...] = _rms(x1, g_ref[...]).astype(BF16)


def _out_proj_router_kernel(yl_ref, yr_ref, x_ref, w_ref, g_ref, rw_ref, x1_ref, hp_ref, idx_ref,
                            gate_ref):
    acc = jnp.dot(yl_ref[...], w_ref[0:D_RNN, :], preferred_element_type=F32)
    acc = acc + jnp.dot(yr_ref[...], w_ref[D_RNN:D_MODEL, :], preferred_element_type=F32)
    x1 = x_ref[...] + acc
    x1_ref[...] = x1
    h = _rms(x1, g_ref[...]).astype(BF16)
    hf = h.astype(F32)
    for a in range(hp_ref.shape[1]):
        hp_ref[:, a, :] = hf[:, a * LANES:(a + 1) * LANES]
    logits = jnp.dot(h, rw_ref[...], preferred_element_type=F32)
    lane_i = lax.broadcasted_iota(jnp.int32, logits.shape, 1)
    lane = lane_i.astype(F32)
    neg = jnp.float32(-jnp.inf)
    logits = jnp.where(lane_i < N_EXPERTS, logits, neg)
    m1 = jnp.max(logits, axis=-1, keepdims=True)
    i1 = jnp.min(jnp.where(logits == m1, lane, float(LANES)), axis=-1, keepdims=True)
    rest = jnp.where(lane == i1, neg, logits)
    m2 = jnp.max(rest, axis=-1, keepdims=True)
    i2 = jnp.min(jnp.where(rest == m2, lane, float(LANES)), axis=-1, keepdims=True)
    e2 = jnp.exp(m2 - m1)
    den = 1.0 + e2
    idx_ref[...] = jnp.where(lane_i == 0, i1, i2).astype(jnp.int32)
    gate_ref[...] = jnp.where(lane_i == 0, 1.0 / den, e2 / den)


def _out_proj(yl, yr, x, w, g, router=None):
    t, d = x.shape
    tm = min(OUT_TM, t)
    row = lambda i: (i, 0)
    fixed = lambda i: (0, 0)
    in_specs = [
        pl.BlockSpec((tm, D_RNN), row),
        pl.BlockSpec((tm, D_RET), row),
        pl.BlockSpec((tm, d), row),
        pl.BlockSpec((d, d), fixed),
        pl.BlockSpec((1, d), fixed),
    ]
    if router is None:
        return pl.pallas_call(
            _out_proj_kernel,
            grid=(t // tm,),
            in_specs=in_specs,
            out_specs=[pl.BlockSpec((tm, d), row), pl.BlockSpec((tm, d), row)],
            out_shape=[jax.ShapeDtypeStruct((t, d), F32), jax.ShapeDtypeStruct((t, d), BF16)],
            compiler_params=_cparams(("parallel",)),
            name="out_proj_norm",
        )(yl, yr, x, w, g)
    return pl.pallas_call(
        _out_proj_router_kernel,
        grid=(t // tm,),
        in_specs=in_specs + [pl.BlockSpec((d, LANES), fixed)],
        out_specs=[pl.BlockSpec((tm, d), row),
                   pl.BlockSpec((tm, d // LANES, LANES), lambda i: (i, 0, 0)),
                   pl.BlockSpec((tm, LANES), row), pl.BlockSpec((tm, LANES), row)],
        out_shape=[jax.ShapeDtypeStruct((t, d), F32),
                   jax.ShapeDtypeStruct((t, d // LANES, LANES), F32),
                   jax.ShapeDtypeStruct((t, LANES), jnp.int32),
                   jax.ShapeDtypeStruct((t, LANES), F32)],
        compiler_params=_cparams(("parallel",)),
        name="out_proj_router",
    )(yl, yr, x, w, g, router)


def _swiglu_step(h, wg, wu, wd):
    gt = jnp.dot(h, wg, preferred_element_type=F32)
    up = jnp.dot(h, wu, preferred_element_type=F32)
    act = (jax.nn.silu(gt) * up).astype(BF16)
    return jnp.dot(act, wd, preferred_element_type=F32)


def _ffn_kernel(h_ref, x_ref, wg_ref, wu_ref, wd_ref, o_ref):
    @pl.when(pl.program_id(1) == 0)
    def _():
        o_ref[...] = x_ref[...]

    o_ref[...] += _swiglu_step(h_ref[...], wg_ref[0], wu_ref[0], wd_ref[...])


def _dense_ffn(h, x, wg, wu, wd, make_casts):
    t, d = x.shape
    tm = min(FFN_TM, t)
    tf = FFN_TF
    grid = (t // tm, D_FF // tf)
    casts = make_casts(*grid)
    in_specs = [
        pl.BlockSpec((tm, d), lambda i, f: (i, 0)),
        pl.BlockSpec((tm, d), lambda i, f: (i, 0)),
        pl.BlockSpec((1, d, tf), lambda i, f: (f, 0, 0)),
        pl.BlockSpec((1, d, tf), lambda i, f: (f, 0, 0)),
        pl.BlockSpec((tf, d), lambda i, f: (f, 0)),
    ]
    outs = pl.pallas_call(
        _with_casts(_ffn_kernel, len(in_specs), 1, len(casts)),
        grid=grid,
        in_specs=in_specs + [c.in_spec for c in casts],
        out_specs=[pl.BlockSpec((tm, d), lambda i, f: (i, 0))] + [c.out_spec for c in casts],
        out_shape=[jax.ShapeDtypeStruct((t, d), F32)] + [c.out_shape for c in casts],
        compiler_params=_cparams(("arbitrary", "arbitrary")),
        name="dense_swiglu",
    )(h, x, wg, wu, wd, *[c.array for c in casts])
    return outs[0], outs[1:]


def _moe_kernel(te_ref, tv_ref, src_ref, nxt_ref, hp_hbm, wg_ref, wu_ref, wd_ref, o_ref,
                xg, wide, hs, sem):
    i = pl.program_id(0)
    f = pl.program_id(1)
    rows = xg.shape[1]
    tm = hs.shape[0]
    per_step = rows // (D_FF // FFN_TF)
    slot = i % 2
    valid = tv_ref[i] == 1

    def start_row(idx_ref, dst_slot, r):
        tok = idx_ref[0, 0, r]
        pltpu.make_async_copy(hp_hbm.at[tok], xg.at[dst_slot, r], sem.at[dst_slot]).start()

    @pl.when((i == 0) & (f == 0))
    def _():
        def body(r, carry):
            start_row(src_ref, 0, r)
            return carry

        lax.fori_loop(0, rows, body, 0, unroll=8)

    @pl.when((f == 0) & ((i == 0) | (tv_ref[jnp.maximum(i - 1, 0)] == 1)))
    def _():
        pltpu.make_async_copy(xg.at[1 - slot], xg.at[slot], sem.at[slot]).wait()

    @pl.when(jnp.logical_not(valid) & (f == 0))
    def _():
        o_ref[...] = jnp.zeros(o_ref.shape, F32)

    @pl.when(valid & (f == 0))
    def _():
        for a in range(xg.shape[2]):
            wide[:, a * LANES:(a + 1) * LANES] = xg[slot, 0:tm, a, :]
        hs[...] = wide[...].astype(BF16)

    def swiglu_part():
        h = hs[...]
        gt = jnp.dot(h, wg_ref[0, 0], preferred_element_type=F32)
        up = jnp.dot(h, wu_ref[0, 0], preferred_element_type=F32)
        for k in range(per_step):
            start_row(nxt_ref, 1 - slot, f * per_step + k)
        act = (jax.nn.silu(gt) * up).astype(BF16)
        return jnp.dot(act, wd_ref[0], preferred_element_type=F32)

    @pl.when(valid & (f == 0))
    def _():
        o_ref[...] = swiglu_part()

    @pl.when(valid & (f > 0))
    def _():
        o_ref[...] += swiglu_part()


def _gather_rows(tm):
    nf = D_FF // FFN_TF
    return nf * pl.cdiv(tm, nf)


def _moe_ffn(hp, tile_expert, tile_valid, src3, wg, wu, wd):
    t = hp.shape[0]
    d = wd.shape[2]
    n_tiles, _, width = src3.shape
    tm = min(FFN_TM, t)
    tf = FFN_TF
    nf = D_FF // tf

    def w_col(i, f, te, tv):
        return (te[i], jnp.where(tv[i] == 1, f, nf - 1), 0, 0)

    def w_row(i, f, te, tv):
        return (te[i], jnp.where(tv[i] == 1, f, nf - 1), 0)

    grid_spec = pltpu.PrefetchScalarGridSpec(
        num_scalar_prefetch=2,
        grid=(n_tiles, nf),
        in_specs=[
            pl.BlockSpec((1, 1, width), lambda i, f, te, tv: (i, 0, 0), memory_space=pltpu.SMEM),
            pl.BlockSpec((1, 1, width),
                         lambda i, f, te, tv: (jnp.minimum(i + 1, n_tiles - 1), 0, 0),
                         memory_space=pltpu.SMEM),
            pl.BlockSpec(memory_space=pl.ANY),
            pl.BlockSpec((1, 1, d, tf), w_col),
            pl.BlockSpec((1, 1, d, tf), w_col),
            pl.BlockSpec((1, tf, d), w_row),
        ],
        out_specs=pl.BlockSpec((tm, d), lambda i, f, te, tv: (i, 0)),
        scratch_shapes=[
            pltpu.VMEM((2, _gather_rows(tm), d // LANES, LANES), F32),
            pltpu.VMEM((tm, d), F32),
            pltpu.VMEM((tm, d), BF16),
            pltpu.SemaphoreType.DMA((2,)),
        ],
    )
    return pl.pallas_call(
        _moe_kernel,
        grid_spec=grid_spec,
        out_shape=jax.ShapeDtypeStruct((n_tiles * tm, d), F32),
        compiler_params=_cparams(("arbitrary", "arbitrary")),
        name="expert_swiglu",
    )(tile_expert, tile_valid, src3, src3, hp, wg, wu, wd)


def _combine_kernel(pos_ref, nxt_ref, x_ref, gate_ref, g_ref, ys_hbm, o_ref, buf, sem):
    i = pl.program_id(0)
    tt = x_ref.shape[0]
    slot = i % 2

    def start_rows(idx_ref, dst_slot):
        def body(grp, carry):
            for sub in range(SUBLANES):
                for k in range(TOP_K):
                    p = idx_ref[0, 0, TOP_K * (grp * SUBLANES + sub) + k]
                    pltpu.make_async_copy(ys_hbm.at[pl.ds(p, 1)],
                                          buf.at[dst_slot, k, grp, pl.ds(sub, 1)],
                                          sem.at[dst_slot]).start()
            return carry

        lax.fori_loop(0, tt // SUBLANES, body, 0)

    @pl.when(i == 0)
    def _():
        start_rows(pos_ref, 0)

    pltpu.make_async_copy(buf.at[1 - slot], buf.at[slot], sem.at[slot]).wait()

    @pl.when(i + 1 < pl.num_programs(0))
    def _():
        start_rows(nxt_ref, 1 - slot)

    d = x_ref.shape[1]
    gates = gate_ref[...]
    moe = gates[:, 0:1] * buf[slot, 0].reshape(tt, d)
    for k in range(1, TOP_K):
        moe = moe + gates[:, k:k + 1] * buf[slot, k].reshape(tt, d)
    o_ref[...] = _rms(x_ref[...] + moe, g_ref[...])


def _combine(x, gates, pos3, ys, g):
    t, d = x.shape
    n_steps, _, per = pos3.shape
    tt = per // TOP_K
    return pl.pallas_call(
        _combine_kernel,
        grid=(n_steps,),
        in_specs=[
            pl.BlockSpec((1, 1, per), lambda i: (i, 0, 0), memory_space=pltpu.SMEM),
            pl.BlockSpec((1, 1, per), lambda i: (jnp.minimum(i + 1, n_steps - 1), 0, 0),
                         memory_space=pltpu.SMEM),
            pl.BlockSpec((tt, d), lambda i: (i, 0)),
            pl.BlockSpec((tt, LANES), lambda i: (i, 0)),
            pl.BlockSpec((1, d), lambda i: (0, 0)),
            pl.BlockSpec(memory_space=pl.ANY),
        ],
        out_specs=pl.BlockSpec((tt, d), lambda i: (i, 0)),
        out_shape=jax.ShapeDtypeStruct((t, d), F32),
        scratch_shapes=[pltpu.VMEM((2, TOP_K, tt // SUBLANES, SUBLANES, d), F32),
                        pltpu.SemaphoreType.DMA((2,))],
        compiler_params=_cparams(("arbitrary",)),
        name="combine_final_norm",
    )(pos3, pos3, x, gates, g, ys)


def _routing_plan(top_idx, tm):
    t = top_idx.shape[0]
    n_slots = t * TOP_K
    e_flat = top_idx.reshape(n_slots)
    onehot = (e_flat[:, None] == jnp.arange(N_EXPERTS, dtype=jnp.int32)[None, :]).astype(jnp.int32)
    csum = jnp.cumsum(onehot, axis=0)
    counts = csum[-1]
    rank = jnp.sum((csum - onehot) * onehot, axis=1)
    padded = ((counts + tm - 1) // tm) * tm
    ends = jnp.cumsum(padded)
    starts = ends - padded
    pos = jnp.sum(starts[None, :] * onehot, axis=1) + rank
    n_tiles = n_slots // tm + N_EXPERTS + 1
    src = jnp.zeros((n_tiles * tm,), jnp.int32).at[pos].set(
        jnp.arange(n_slots, dtype=jnp.int32) // TOP_K)
    width = LANES * pl.cdiv(_gather_rows(tm), LANES)
    src = jnp.pad(src.reshape(n_tiles, tm), ((0, 0), (0, width - tm)))
    tile_start = jnp.arange(n_tiles, dtype=jnp.int32) * tm
    tile_expert = jnp.minimum(
        jnp.sum((tile_start[:, None] >= ends[None, :]).astype(jnp.int32), axis=1), N_EXPERTS - 1)
    tile_valid = (tile_start < ends[-1]).astype(jnp.int32)
    return pos, src.reshape(n_tiles, 1, width), tile_expert, tile_valid


def kernel(x, norm1_g, w_in, conv_w, conv_b, lru_wa, lru_ba, lru_wx, lru_bx, lru_lambda,
           lru_norm_g, ret_norm_g, w_out, norm2_g, ffn_w_gate, ffn_w_up, ffn_w_down,
           moe_router, moe_w_gate, moe_w_up, moe_w_down, final_norm_g):
    assert DEPTH == 2 and w_in.shape[0] == DEPTH
    b, s, d = x.shape
    t = b * s
    tables = _ret_tables(s)
    xf = x.reshape(t, d)
    row = lambda v: v.reshape(1, -1)

    def branches(layer, x_in, w_in_bf16, lru_casts):
        proj = _norm_matmul(x_in, row(norm1_g[layer]), w_in_bf16)
        proj3 = proj.reshape(b, s, D_IN)
        y_lru, lru_cast = _lru_branch(proj3, conv_w[layer], row(conv_b[layer]),
                                      lru_wa[layer].astype(BF16), row(lru_ba[layer]),
                                      lru_wx[layer].astype(BF16), row(lru_bx[layer]),
                                      row(lru_lambda[layer]), row(lru_norm_g[layer]), lru_casts)
        y_ret = _ret_branch(proj3, tables, row(ret_norm_g[layer]))
        return y_lru.reshape(t, D_RNN), y_ret.reshape(t, D_RET), lru_cast

    flat2 = lambda w: w.reshape(-1, w.shape[-1])
    inner = s // min(SEQ_TS, s)
    n_steps = b * inner
    step = lambda i, j: i * inner + j
    yl, yr, (wg0, wu0, wd0, wo0, wi1, wo1) = branches(
        0, xf, w_in[0].astype(BF16),
        [_tiled_cast(ffn_w_gate[0], 1, n_steps, step), _tiled_cast(ffn_w_up[0], 1, n_steps, step),
         _flat_cast(ffn_w_down[0], n_steps, step),
         _flat_cast(flat2(w_out), n_steps, step, 0, DEPTH),
         _flat_cast(flat2(w_in), n_steps, step, 1, DEPTH),
         _flat_cast(flat2(w_out), n_steps, step, 1, DEPTH)])
    x1, h2 = _out_proj(yl, yr, xf, wo0, row(norm2_g[0]))
    xf, (moe_g, moe_d) = _dense_ffn(
        h2, x1, wg0[0], wu0[0], wd0,
        lambda ni, nf: [_grid_tiled_cast(flat2(moe_w_gate[0]), N_EXPERTS, ni),
                        _flat_cast(flat2(moe_w_down[0]), ni * nf, lambda i, f: i * nf + f)])
    yl, yr, (moe_u,) = branches(
        1, xf, wi1, [_tiled_cast(flat2(moe_w_up[0]), N_EXPERTS, n_steps, step)])
    router = jnp.pad(moe_router[0], ((0, 0), (0, LANES - N_EXPERTS))).astype(BF16)
    x1, hp, idx, gates = _out_proj(yl, yr, xf, wo1, row(norm2_g[1]), router)
    tm = min(FFN_TM, t)
    pos, src3, tile_expert, tile_valid = _routing_plan(idx[:, :TOP_K], tm)
    ys = _moe_ffn(hp, tile_expert, tile_valid, src3,
                  moe_g, moe_u, moe_d.reshape(moe_w_down.shape[1:]))
    tt = min(COMB_TT, t)
    pos3 = pos.reshape(t // tt, 1, tt * TOP_K)
    out = _combine(x1, gates, pos3, ys, row(final_norm_g))
    return out.reshape(b, s, d)
```

```python
import functools

import jax
import jax.numpy as jnp
from jax import lax
from jax.experimental import pallas as pl
from jax.experimental.pallas import tpu as pltpu

F32 = jnp.float32
BF16 = jnp.bfloat16

D_MODEL = 2048
DEPTH = 2
D_RNN = D_MODEL // 2
D_RET = D_MODEL - D_RNN
LRU_BLOCKS = 8
LRU_BLOCK = D_RNN // LRU_BLOCKS
CONV_WIDTH = 4
LRU_C = 8.0
RET_HEADS = 8
RET_DV = D_RET // RET_HEADS
RET_DK = RET_DV // 2
RET_CHUNK = 128
ROPE_BASE = 10000.0
D_FF = 256 * ((8 * D_MODEL // 3 + 255) // 256)
N_EXPERTS = 8
TOP_K = 2
EPS = 1e-6
LOG2_E = 1.4426950408889634
D_QK = RET_HEADS * RET_DK
D_IN = 2 * D_RNN + 2 * D_QK + 2 * D_RET

LANES = 128
SUBLANES = 8
VMEM_LIMIT_BYTES = 56 * 1024 * 1024

PROJ_TM = 1024
PROJ_TN = 1280
SEQ_TS = 256
OUT_TM = 512
FFN_TM = 512
FFN_TF = 512
COMB_TT = 256


def _cparams(sem):
    return pltpu.CompilerParams(dimension_semantics=sem, vmem_limit_bytes=VMEM_LIMIT_BYTES)


def _rms(x, g):
    return x * lax.rsqrt(jnp.mean(x * x, axis=-1, keepdims=True) + EPS) * g


def _with_casts(body, n_in, n_out, n_cast):
    def kern(*refs):
        ins = refs[:n_in]
        cast_in = refs[n_in:n_in + n_cast]
        outs = refs[n_in + n_cast:n_in + n_cast + n_out]
        cast_out = refs[n_in + n_cast + n_out:n_in + 2 * n_cast + n_out]
        scratch = refs[n_in + 2 * n_cast + n_out:]
        body(*ins, *outs, *scratch)
        for src, dst in zip(cast_in, cast_out):
            _cast_block(src, dst)

    return kern


def _cast_block(src, dst):
    if len(dst.shape) == 2:
        dst[...] = src[...].astype(BF16)
        return
    n, _, tf = dst.shape[-3:]
    lead = (0,) * (len(dst.shape) - 3)
    for f in range(n):
        dst[lead + (f,)] = src[:, f * tf:(f + 1) * tf].astype(BF16)


class _Cast:
    def __init__(self, array, in_spec, out_spec, out_shape):
        self.array, self.in_spec, self.out_spec = array, in_spec, out_spec
        self.out_shape = jax.ShapeDtypeStruct(out_shape, BF16)


def _flat_cast(a, n_steps, step_of, part=0, n_parts=1):
    rows_all, cols = a.shape
    rows = rows_all // n_parts
    blk = rows // n_steps
    in_spec = pl.BlockSpec((blk, cols), lambda *ids: (part * n_steps + step_of(*ids), 0))
    out_spec = pl.BlockSpec((blk, cols), lambda *ids: (step_of(*ids), 0))
    return _Cast(a, in_spec, out_spec, (rows, cols))


def _tiled_cast(a, n_experts, n_steps, step_of):
    rows_all, cols = a.shape
    rows = rows_all // n_experts
    blk = rows_all // n_steps
    per = rows // blk
    in_spec = pl.BlockSpec((blk, cols), lambda *ids: (step_of(*ids), 0))
    out_spec = pl.BlockSpec((1, cols // FFN_TF, blk, FFN_TF),
                            lambda *ids: (step_of(*ids) // per, 0, step_of(*ids) % per, 0))
    return _Cast(a, in_spec, out_spec, (n_experts, cols // FFN_TF, rows, FFN_TF))


def _grid_tiled_cast(a, n_experts, n_row_steps):
    rows_all, cols = a.shape
    rows = rows_all // n_experts
    blk = rows_all // n_row_steps
    per = rows // blk
    in_spec = pl.BlockSpec((blk, FFN_TF), lambda i, f: (i, f))
    out_spec = pl.BlockSpec((1, 1, blk, FFN_TF), lambda i, f: (i // per, f, i % per, 0))
    return _Cast(a, in_spec, out_spec, (n_experts, cols // FFN_TF, rows, FFN_TF))


def _norm_matmul_kernel(x_ref, g_ref, w_ref, o_ref, h_ref):
    @pl.when(pl.program_id(1) == 0)
    def _():
        h_ref[...] = _rms(x_ref[...], g_ref[...]).astype(BF16)

    o_ref[...] = jnp.dot(h_ref[...], w_ref[...], preferred_element_type=F32)


def _norm_matmul(x, g, w):
    t, d = x.shape
    n = w.shape[1]
    tm = min(PROJ_TM, t)
    tn = PROJ_TN
    return pl.pallas_call(
        _norm_matmul_kernel,
        grid=(t // tm, n // tn),
        in_specs=[
            pl.BlockSpec((tm, d), lambda i, j: (i, 0)),
            pl.BlockSpec((1, d), lambda i, j: (0, 0)),
            pl.BlockSpec((d, tn), lambda i, j: (0, j)),
        ],
        out_specs=pl.BlockSpec((tm, tn), lambda i, j: (i, j)),
        out_shape=jax.ShapeDtypeStruct((t, n), F32),
        scratch_shapes=[pltpu.VMEM((tm, d), BF16)],
        compiler_params=_cparams(("parallel", "arbitrary")),
        name="norm_in_proj",
    )(x, g, w)


def _lru_kernel(x_ref, gate_ref, cw_ref, cb_ref, wa_ref, ba_ref, wx_ref, bx_ref, lam_ref,
                ng_ref, o_ref, xbuf, a_s, b_s, hcar):
    c = pl.program_id(1)
    ts = x_ref.shape[1]
    hist = SUBLANES

    @pl.when(c == 0)
    def _():
        xbuf[...] = jnp.zeros((hist, D_RNN), F32)
        hcar[...] = jnp.zeros((1, D_RNN), F32)

    x = x_ref[0]
    prev = xbuf[...]
    rows8 = lax.broadcasted_iota(jnp.int32, (SUBLANES, D_RNN), 0)
    xc = cb_ref[...]
    for tap in range(CONV_WIDTH):
        back = CONV_WIDTH - 1 - tap
        if back == 0:
            xs = x
        else:
            rolled = pltpu.roll(x, back, axis=0)
            head = jnp.where(rows8 < back, pltpu.roll(prev, back, axis=0), rolled[0:hist, :])
            xs = jnp.concatenate([head, rolled[hist:, :]], axis=0)
        xc = xc + xs * cw_ref[tap:tap + 1, :]
    xbuf[...] = x[ts - hist:ts, :]

    xcb = xc.astype(BF16)
    ra = []
    rx = []
    for n in range(LRU_BLOCKS):
        xs = xcb[:, n * LRU_BLOCK:(n + 1) * LRU_BLOCK]
        ra.append(jnp.dot(xs, wa_ref[n], preferred_element_type=F32))
        rx.append(jnp.dot(xs, wx_ref[n], preferred_element_type=F32))
    r = jax.nn.sigmoid(jnp.concatenate(ra, axis=1) + ba_ref[...])
    ig = jax.nn.sigmoid(jnp.concatenate(rx, axis=1) + bx_ref[...])

    z = -lam_ref[...]
    softplus = jnp.maximum(z, 0.0) + jnp.log1p(jnp.exp(-jnp.abs(z)))
    a = jnp.exp2(r * ((-LRU_C * LOG2_E) * softplus))
    om = 1.0 - a * a
    mult = jnp.where(om > 0.0, om * lax.rsqrt(om), 0.0)
    a_s[...] = a
    b_s[...] = mult * ig * xc

    @pl.when(c == 0)
    def _():
        b_s[0:1, :] = ig[0:1, :] * xc[0:1, :]

    def body(i, h):
        r0 = pl.multiple_of(i * SUBLANES, SUBLANES)
        av = a_s[pl.ds(r0, SUBLANES), :]
        bv = b_s[pl.ds(r0, SUBLANES), :]
        for sh in (1, 2, 4):
            a_sh = pltpu.roll(av, sh, axis=0)
            b_sh = pltpu.roll(bv, sh, axis=0)
            m = rows8 >= sh
            bv = jnp.where(m, av * b_sh + bv, bv)
            av = jnp.where(m, av * a_sh, av)
        hv = av * h + bv
        b_s[pl.ds(r0, SUBLANES), :] = hv
        return hv[SUBLANES - 1:SUBLANES, :]

    hcar[...] = lax.fori_loop(0, ts // SUBLANES, body, hcar[...])

    y = _rms(b_s[...], ng_ref[...])
    o_ref[0] = (y * jax.nn.gelu(gate_ref[0])).astype(BF16)


def _lru_branch(proj3, cw, cb, wa, ba, wx, bx, lam, ng, casts):
    b, s, _ = proj3.shape
    ts = min(SEQ_TS, s)
    vec = pl.BlockSpec((1, D_RNN), lambda i, j: (0, 0))
    blk = pl.BlockSpec((LRU_BLOCKS, LRU_BLOCK, LRU_BLOCK), lambda i, j: (0, 0, 0))
    in_specs = [
        pl.BlockSpec((1, ts, D_RNN), lambda i, j: (i, j, 0)),
        pl.BlockSpec((1, ts, D_RNN), lambda i, j: (i, j, 1)),
        pl.BlockSpec((CONV_WIDTH, D_RNN), lambda i, j: (0, 0)),
        vec, blk, vec, blk, vec, vec, vec,
    ]
    outs = pl.pallas_call(
        _with_casts(_lru_kernel, len(in_specs), 1, len(casts)),
        grid=(b, s // ts),
        in_specs=in_specs + [c.in_spec for c in casts],
        out_specs=[pl.BlockSpec((1, ts, D_RNN), lambda i, j: (i, j, 0))]
        + [c.out_spec for c in casts],
        out_shape=[jax.ShapeDtypeStruct((b, s, D_RNN), BF16)] + [c.out_shape for c in casts],
        scratch_shapes=[
            pltpu.VMEM((SUBLANES, D_RNN), F32),
            pltpu.VMEM((ts, D_RNN), F32),
            pltpu.VMEM((ts, D_RNN), F32),
            pltpu.VMEM((1, D_RNN), F32),
        ],
        compiler_params=_cparams(("arbitrary", "arbitrary")),
        name="rglru_branch",
    )(proj3, proj3, cw, cb, wa, ba, wx, bx, lam, ng, *[c.array for c in casts])
    return outs[0], outs[1:]


def _rope(x, cos, sin_signed):
    half = RET_DK // 2
    lane = lax.broadcasted_iota(jnp.int32, x.shape, 1)
    first = (lane & (RET_DK - 1)) < half
    rot = jnp.where(first, pltpu.roll(x, LANES - half, axis=1), pltpu.roll(x, half, axis=1))
    return x * cos + rot * sin_signed


def _ret_kernel(q_ref, k_ref, v_ref, g_ref, cos_ref, sin_ref, xim_ref, zmt_ref, dec_ref,
                cd_ref, gn_ref, o_ref, state):
    c = pl.program_id(1)
    ts = q_ref.shape[1]

    @pl.when(c == 0)
    def _():
        state[...] = jnp.zeros(state.shape, F32)

    lane = lax.broadcasted_iota(jnp.int32, (RET_CHUNK, LANES), 1)
    even_head = lane < RET_DK
    for p in range(RET_HEADS // 2):
        ps = slice(p * LANES, (p + 1) * LANES)
        for n in range(ts // RET_CHUNK):
            rows = slice(n * RET_CHUNK, (n + 1) * RET_CHUNK)
            cos = cos_ref[rows, :]
            sin = sin_ref[rows, :]
            qp = _rope(q_ref[0, rows, ps], cos, sin)
            kt = _rope(k_ref[0, rows, ps], cos, sin).T
            ktb = kt.astype(BF16)
            qm = jnp.concatenate([jnp.where(even_head, qp, 0.0), jnp.where(even_head, 0.0, qp)],
                                 axis=0).astype(BF16)
            sc = jnp.dot(qm, ktb, preferred_element_type=F32)
            for par in range(2):
                h = 2 * p + par
                vs = slice(h * RET_DV, (h + 1) * RET_DV)
                vh = v_ref[0, rows, vs].astype(BF16)
                scores = sc[par * RET_CHUNK:(par + 1) * RET_CHUNK, :] * dec_ref[h]
                inner = jnp.dot(scores.astype(BF16), vh, preferred_element_type=F32)
                st = state[h]
                cross = jnp.dot((qp * xim_ref[h]).astype(BF16), st.astype(BF16),
                                preferred_element_type=F32)
                kv = jnp.dot((kt * zmt_ref[h]).astype(BF16), vh, preferred_element_type=F32)
                state[h] = cd_ref[h] * st + kv
                o = inner + cross
                mu = jnp.mean(o, axis=-1, keepdims=True)
                d = o - mu
                var = jnp.mean(d * d, axis=-1, keepdims=True)
                y = d * lax.rsqrt(var + EPS) * gn_ref[:, vs]
                o_ref[0, rows, vs] = (y * jax.nn.silu(g_ref[0, rows, vs])).astype(BF16)


def _ret_tables(s):
    scale = RET_DK ** -0.5
    inv_freq = ROPE_BASE ** (-jnp.arange(0, RET_DK, 2, dtype=F32) / RET_DK)
    ang = jnp.arange(s, dtype=F32)[:, None] * inv_freq[None, :]
    cos = jnp.cos(ang)
    sin = jnp.sin(ang)
    cos_t = jnp.tile(cos, (1, 2 * LANES // RET_DK))
    sin_t = jnp.tile(jnp.concatenate([-sin, sin], axis=1), (1, LANES // RET_DK))
    log_g = jnp.log(1.0 - 2.0 ** (-5.0 - jnp.arange(RET_HEADS, dtype=F32)))
    idx = jnp.arange(RET_CHUNK, dtype=F32)
    rel = idx[:, None] - idx[None, :]
    decay = jnp.where(rel >= 0, jnp.exp(jnp.maximum(rel, 0.0)[None] * log_g[:, None, None]), 0.0)
    zeta = jnp.exp((RET_CHUNK - 1.0 - idx)[None, :] * log_g[:, None])
    xi = jnp.exp((idx + 1.0)[None, :] * log_g[:, None])
    own = (jnp.arange(LANES)[None, :] // RET_DK) == (jnp.arange(RET_HEADS)[:, None] % 2)
    xim = jnp.where(own[:, None, :], xi[:, :, None], 0.0)
    zmt = jnp.where(own[:, :, None], scale * zeta[:, None, :], 0.0)
    cdec = jnp.exp(RET_CHUNK * log_g)
    return cos_t, sin_t, xim, zmt, decay * scale, cdec


def _ret_branch(proj3, tables, gn):
    b, s, _ = proj3.shape
    ts = min(SEQ_TS, s)
    cos_t, sin_t, xim, zmt, decay, cdec = tables
    q_blk = 2 * D_RNN // D_QK
    per_head = pl.BlockSpec((RET_HEADS, RET_CHUNK, RET_CHUNK), lambda i, j: (0, 0, 0))
    assert RET_CHUNK == LANES == RET_DV == 2 * RET_DK
    in_specs = [
        pl.BlockSpec((1, ts, D_QK), lambda i, j: (i, j, q_blk)),
        pl.BlockSpec((1, ts, D_QK), lambda i, j: (i, j, q_blk + 1)),
        pl.BlockSpec((1, ts, D_RET), lambda i, j: (i, j, 3)),
        pl.BlockSpec((1, ts, D_RET), lambda i, j: (i, j, 4)),
        pl.BlockSpec((ts, LANES), lambda i, j: (j, 0)),
        pl.BlockSpec((ts, LANES), lambda i, j: (j, 0)),
        per_head, per_head, per_head,
        pl.BlockSpec(memory_space=pltpu.SMEM),
        pl.BlockSpec((1, D_RET), lambda i, j: (0, 0)),
    ]
    return pl.pallas_call(
        _ret_kernel,
        grid=(b, s // ts),
        in_specs=in_specs,
        out_specs=pl.BlockSpec((1, ts, D_RET), lambda i, j: (i, j, 0)),
        out_shape=jax.ShapeDtypeStruct((b, s, D_RET), BF16),
        scratch_shapes=[pltpu.VMEM((RET_HEADS, LANES, RET_DV), F32)],
        compiler_params=_cparams(("parallel", "arbitrary")),
        name="retention_branch",
    )(proj3, proj3, proj3, proj3, cos_t, sin_t, xim, zmt, decay, cdec, gn)


def _out_proj_kernel(yl_ref, yr_ref, x_ref, w_ref, g_ref, x1_ref, h_ref):
    acc = jnp.dot(yl_ref[...], w_ref[0:D_RNN, :], preferred_element_type=F32)
    acc = acc + jnp.dot(yr_ref[...], w_ref[D_RNN:D_MODEL, :], preferred_element_type=F32)
    x1 = x_ref[...] + acc
    x1_ref[...] = x1
    h_ref[...] = _rms(x1, g_ref[...]).astype(BF16)


def _out_proj_router_kernel(yl_ref, yr_ref, x_ref, w_ref, g_ref, rw_ref, x1_ref, hp_ref, idx_ref,
                            gate_ref):
    acc = jnp.dot(yl_ref[...], w_ref[0:D_RNN, :], preferred_element_type=F32)
    acc = acc + jnp.dot(yr_ref[...], w_ref[D_RNN:D_MODEL, :], preferred_element_type=F32)
    x1 = x_ref[...] + acc
    x1_ref[...] = x1
    h = _rms(x1, g_ref[...]).astype(BF16)
    hf = h.astype(F32)
    for a in range(hp_ref.shape[1]):
        hp_ref[:, a, :] = hf[:, a * LANES:(a + 1) * LANES]
    logits = jnp.dot(h, rw_ref[...], preferred_element_type=F32)
    lane_i = lax.broadcasted_iota(jnp.int32, logits.shape, 1)
    lane = lane_i.astype(F32)
    neg = jnp.float32(-jnp.inf)
    logits = jnp.where(lane_i < N_EXPERTS, logits, neg)
    m1 = jnp.max(logits, axis=-1, keepdims=True)
    i1 = jnp.min(jnp.where(logits == m1, lane, float(LANES)), axis=-1, keepdims=True)
    rest = jnp.where(lane == i1, neg, logits)
    m2 = jnp.max(rest, axis=-1, keepdims=True)
    i2 = jnp.min(jnp.where(rest == m2, lane, float(LANES)), axis=-1, keepdims=True)
    e2 = jnp.exp(m2 - m1)
    den = 1.0 + e2
    idx_ref[...] = jnp.where(lane_i == 0, i1, i2).astype(jnp.int32)
    gate_ref[...] = jnp.where(lane_i == 0, 1.0 / den, e2 / den)


def _out_proj(yl, yr, x, w, g, router=None):
    t, d = x.shape
    tm = min(OUT_TM, t)
    row = lambda i: (i, 0)
    fixed = lambda i: (0, 0)
    in_specs = [
        pl.BlockSpec((tm, D_RNN), row),
        pl.BlockSpec((tm, D_RET), row),
        pl.BlockSpec((tm, d), row),
        pl.BlockSpec((d, d), fixed),
        pl.BlockSpec((1, d), fixed),
    ]
    if router is None:
        return pl.pallas_call(
            _out_proj_kernel,
            grid=(t // tm,),
            in_specs=in_specs,
            out_specs=[pl.BlockSpec((tm, d), row), pl.BlockSpec((tm, d), row)],
            out_shape=[jax.ShapeDtypeStruct((t, d), F32), jax.ShapeDtypeStruct((t, d), BF16)],
            compiler_params=_cparams(("parallel",)),
            name="out_proj_norm",
        )(yl, yr, x, w, g)
    return pl.pallas_call(
        _out_proj_router_kernel,
        grid=(t // tm,),
        in_specs=in_specs + [pl.BlockSpec((d, LANES), fixed)],
        out_specs=[pl.BlockSpec((tm, d), row),
                   pl.BlockSpec((tm, d // LANES, LANES), lambda i: (i, 0, 0)),
                   pl.BlockSpec((tm, LANES), row), pl.BlockSpec((tm, LANES), row)],
        out_shape=[jax.ShapeDtypeStruct((t, d), F32),
                   jax.ShapeDtypeStruct((t, d // LANES, LANES), F32),
                   jax.ShapeDtypeStruct((t, LANES), jnp.int32),
                   jax.ShapeDtypeStruct((t, LANES), F32)],
        compiler_params=_cparams(("parallel",)),
        name="out_proj_router",
    )(yl, yr, x, w, g, router)


def _swiglu_step(h, wg, wu, wd):
    gt = jnp.dot(h, wg, preferred_element_type=F32)
    up = jnp.dot(h, wu, preferred_element_type=F32)
    act = (jax.nn.silu(gt) * up).astype(BF16)
    return jnp.dot(act, wd, preferred_element_type=F32)


def _ffn_kernel(h_ref, x_ref, wg_ref, wu_ref, wd_ref, o_ref):
    @pl.when(pl.program_id(1) == 0)
    def _():
        o_ref[...] = x_ref[...]

    o_ref[...] += _swiglu_step(h_ref[...], wg_ref[0], wu_ref[0], wd_ref[...])


def _dense_ffn(h, x, wg, wu, wd, make_casts):
    t, d = x.shape
    tm = min(FFN_TM, t)
    tf = FFN_TF
    grid = (t // tm, D_FF // tf)
    casts = make_casts(*grid)
    in_specs = [
        pl.BlockSpec((tm, d), lambda i, f: (i, 0)),
        pl.BlockSpec((tm, d), lambda i, f: (i, 0)),
        pl.BlockSpec((1, d, tf), lambda i, f: (f, 0, 0)),
        pl.BlockSpec((1, d, tf), lambda i, f: (f, 0, 0)),
        pl.BlockSpec((tf, d), lambda i, f: (f, 0)),
    ]
    outs = pl.pallas_call(
        _with_casts(_ffn_kernel, len(in_specs), 1, len(casts)),
        grid=grid,
        in_specs=in_specs + [c.in_spec for c in casts],
        out_specs=[pl.BlockSpec((tm, d), lambda i, f: (i, 0))] + [c.out_spec for c in casts],
        out_shape=[jax.ShapeDtypeStruct((t, d), F32)] + [c.out_shape for c in casts],
        compiler_params=_cparams(("arbitrary", "arbitrary")),
        name="dense_swiglu",
    )(h, x, wg, wu, wd, *[c.array for c in casts])
    return outs[0], outs[1:]


def _moe_kernel(te_ref, tv_ref, src_ref, nxt_ref, hp_hbm, wg_ref, wu_ref, wd_ref, o_ref,
                xg, wide, hs, sem):
    i = pl.program_id(0)
    f = pl.program_id(1)
    rows = xg.shape[1]
    tm = hs.shape[0]
    per_step = rows // (D_FF // FFN_TF)
    slot = i % 2
    valid = tv_ref[i] == 1

    def start_row(idx_ref, dst_slot, r):
        tok = idx_ref[0, 0, r]
        pltpu.make_async_copy(hp_hbm.at[tok], xg.at[dst_slot, r], sem.at[dst_slot]).start()

    @pl.when((i == 0) & (f == 0))
    def _():
        def body(r, carry):
            start_row(src_ref, 0, r)
            return carry

        lax.fori_loop(0, rows, body, 0, unroll=8)

    @pl.when((f == 0) & ((i == 0) | (tv_ref[jnp.maximum(i - 1, 0)] == 1)))
    def _():
        pltpu.make_async_copy(xg.at[1 - slot], xg.at[slot], sem.at[slot]).wait()

    @pl.when(jnp.logical_not(valid) & (f == 0))
    def _():
        o_ref[...] = jnp.zeros(o_ref.shape, F32)

    @pl.when(valid & (f == 0))
    def _():
        for a in range(xg.shape[2]):
            wide[:, a * LANES:(a + 1) * LANES] = xg[slot, 0:tm, a, :]
        hs[...] = wide[...].astype(BF16)

    def swiglu_part():
        h = hs[...]
        gt = jnp.dot(h, wg_ref[0, 0], preferred_element_type=F32)
        up = jnp.dot(h, wu_ref[0, 0], preferred_element_type=F32)
        for k in range(per_step):
            start_row(nxt_ref, 1 - slot, f * per_step + k)
        act = (jax.nn.silu(gt) * up).astype(BF16)
        return jnp.dot(act, wd_ref[0], preferred_element_type=F32)

    @pl.when(valid & (f == 0))
    def _():
        o_ref[...] = swiglu_part()

    @pl.when(valid & (f > 0))
    def _():
        o_ref[...] += swiglu_part()


def _gather_rows(tm):
    nf = D_FF // FFN_TF
    return nf * pl.cdiv(tm, nf)


def _moe_ffn(hp, tile_expert, tile_valid, src3, wg, wu, wd):
    t = hp.shape[0]
    d = wd.shape[2]
    n_tiles, _, width = src3.shape
    tm = min(FFN_TM, t)
    tf = FFN_TF
    nf = D_FF // tf

    def w_col(i, f, te, tv):
        return (te[i], jnp.where(tv[i] == 1, f, nf - 1), 0, 0)

    def w_row(i, f, te, tv):
        return (te[i], jnp.where(tv[i] == 1, f, nf - 1), 0)

    grid_spec = pltpu.PrefetchScalarGridSpec(
        num_scalar_prefetch=2,
        grid=(n_tiles, nf),
        in_specs=[
            pl.BlockSpec((1, 1, width), lambda i, f, te, tv: (i, 0, 0), memory_space=pltpu.SMEM),
            pl.BlockSpec((1, 1, width),
                         lambda i, f, te, tv: (jnp.minimum(i + 1, n_tiles - 1), 0, 0),
                         memory_space=pltpu.SMEM),
            pl.BlockSpec(memory_space=pl.ANY),
            pl.BlockSpec((1, 1, d, tf), w_col),
            pl.BlockSpec((1, 1, d, tf), w_col),
            pl.BlockSpec((1, tf, d), w_row),
        ],
        out_specs=pl.BlockSpec((tm, d), lambda i, f, te, tv: (i, 0)),
        scratch_shapes=[
            pltpu.VMEM((2, _gather_rows(tm), d // LANES, LANES), F32),
            pltpu.VMEM((tm, d), F32),
            pltpu.VMEM((tm, d), BF16),
            pltpu.SemaphoreType.DMA((2,)),
        ],
    )
    return pl.pallas_call(
        _moe_kernel,
        grid_spec=grid_spec,
        out_shape=jax.ShapeDtypeStruct((n_tiles * tm, d), F32),
        compiler_params=_cparams(("arbitrary", "arbitrary")),
        name="expert_swiglu",
    )(tile_expert, tile_valid, src3, src3, hp, wg, wu, wd)


def _combine_kernel(pos_ref, nxt_ref, x_ref, gate_ref, g_ref, ys_hbm, o_ref, buf, sem):
    i = pl.program_id(0)
    tt = x_ref.shape[0]
    slot = i % 2

    def start_rows(idx_ref, dst_slot):
        def body(grp, carry):
            for sub in range(SUBLANES):
                for k in range(TOP_K):
                    p = idx_ref[0, 0, TOP_K * (grp * SUBLANES + sub) + k]
                    pltpu.make_async_copy(ys_hbm.at[pl.ds(p, 1)],
                                          buf.at[dst_slot, k, grp, pl.ds(sub, 1)],
                                          sem.at[dst_slot]).start()
            return carry

        lax.fori_loop(0, tt // SUBLANES, body, 0)

    @pl.when(i == 0)
    def _():
        start_rows(pos_ref, 0)

    pltpu.make_async_copy(buf.at[1 - slot], buf.at[slot], sem.at[slot]).wait()

    @pl.when(i + 1 < pl.num_programs(0))
    def _():
        start_rows(nxt_ref, 1 - slot)

    d = x_ref.shape[1]
    gates = gate_ref[...]
    moe = gates[:, 0:1] * buf[slot, 0].reshape(tt, d)
    for k in range(1, TOP_K):
        moe = moe + gates[:, k:k + 1] * buf[slot, k].reshape(tt, d)
    o_ref[...] = _rms(x_ref[...] + moe, g_ref[...])


def _combine(x, gates, pos3, ys, g):
    t, d = x.shape
    n_steps, _, per = pos3.shape
    tt = per // TOP_K
    return pl.pallas_call(
        _combine_kernel,
        grid=(n_steps,),
        in_specs=[
            pl.BlockSpec((1, 1, per), lambda i: (i, 0, 0), memory_space=pltpu.SMEM),
            pl.BlockSpec((1, 1, per), lambda i: (jnp.minimum(i + 1, n_steps - 1), 0, 0),
                         memory_space=pltpu.SMEM),
            pl.BlockSpec((tt, d), lambda i: (i, 0)),
            pl.BlockSpec((tt, LANES), lambda i: (i, 0)),
            pl.BlockSpec((1, d), lambda i: (0, 0)),
            pl.BlockSpec(memory_space=pl.ANY),
        ],
        out_specs=pl.BlockSpec((tt, d), lambda i: (i, 0)),
        out_shape=jax.ShapeDtypeStruct((t, d), F32),
        scratch_shapes=[pltpu.VMEM((2, TOP_K, tt // SUBLANES, SUBLANES, d), F32),
                        pltpu.SemaphoreType.DMA((2,))],
        compiler_params=_cparams(("arbitrary",)),
        name="combine_final_norm",
    )(pos3, pos3, x, gates, g, ys)


def _routing_plan(top_idx, tm):
    t = top_idx.shape[0]
    n_slots = t * TOP_K
    e_flat = top_idx.reshape(n_slots)
    onehot = (e_flat[:, None] == jnp.arange(N_EXPERTS, dtype=jnp.int32)[None, :]).astype(jnp.int32)
    csum = jnp.cumsum(onehot, axis=0)
    counts = csum[-1]
    rank = jnp.sum((csum - onehot) * onehot, axis=1)
    padded = ((counts + tm - 1) // tm) * tm
    ends = jnp.cumsum(padded)
    starts = ends - padded
    pos = jnp.sum(starts[None, :] * onehot, axis=1) + rank
    n_tiles = n_slots // tm + N_EXPERTS + 1
    src = jnp.zeros((n_tiles * tm,), jnp.int32).at[pos].set(
        jnp.arange(n_slots, dtype=jnp.int32) // TOP_K)
    width = LANES * pl.cdiv(_gather_rows(tm), LANES)
    src = jnp.pad(src.reshape(n_tiles, tm), ((0, 0), (0, width - tm)))
    tile_start = jnp.arange(n_tiles, dtype=jnp.int32) * tm
    tile_expert = jnp.minimum(
        jnp.sum((tile_start[:, None] >= ends[None, :]).astype(jnp.int32), axis=1), N_EXPERTS - 1)
    tile_valid = (tile_start < ends[-1]).astype(jnp.int32)
    return pos, src.reshape(n_tiles, 1, width), tile_expert, tile_valid


def kernel(x, norm1_g, w_in, conv_w, conv_b, lru_wa, lru_ba, lru_wx, lru_bx, lru_lambda,
           lru_norm_g, ret_norm_g, w_out, norm2_g, ffn_w_gate, ffn_w_up, ffn_w_down,
           moe_router, moe_w_gate, moe_w_up, moe_w_down, final_norm_g):
    assert DEPTH == 2 and w_in.shape[0] == DEPTH
    b, s, d = x.shape
    t = b * s
    tables = _ret_tables(s)
    xf = x.reshape(t, d)
    row = lambda v: v.reshape(1, -1)

    def branches(layer, x_in, w_in_bf16, lru_casts):
        proj = _norm_matmul(x_in, row(norm1_g[layer]), w_in_bf16)
        proj3 = proj.reshape(b, s, D_IN)
        y_lru, lru_cast = _lru_branch(proj3, conv_w[layer], row(conv_b[layer]),
                                      lru_wa[layer].astype(BF16), row(lru_ba[layer]),
                                      lru_wx[layer].astype(BF16), row(lru_bx[layer]),
                                      row(lru_lambda[layer]), row(lru_norm_g[layer]), lru_casts)
        y_ret = _ret_branch(proj3, tables, row(ret_norm_g[layer]))
        return y_lru.reshape(t, D_RNN), y_ret.reshape(t, D_RET), lru_cast

    flat2 = lambda w: w.reshape(-1, w.shape[-1])
    inner = s // min(SEQ_TS, s)
    n_steps = b * inner
    step = lambda i, j: i * inner + j
    yl, yr, (wg0, wu0, wd0, wo0, wi1, wo1) = branches(
        0, xf, w_in[0].astype(BF16),
        [_tiled_cast(ffn_w_gate[0], 1, n_steps, step), _tiled_cast(ffn_w_up[0], 1, n_steps, step),
         _flat_cast(ffn_w_down[0], n_steps, step),
         _flat_cast(flat2(w_out), n_steps, step, 0, DEPTH),
         _flat_cast(flat2(w_in), n_steps, step, 1, DEPTH),
         _flat_cast(flat2(w_out), n_steps, step, 1, DEPTH)])
    x1, h2 = _out_proj(yl, yr, xf, wo0, row(norm2_g[0]))
    xf, (moe_g, moe_d) = _dense_ffn(
        h2, x1, wg0[0], wu0[0], wd0,
        lambda ni, nf: [_grid_tiled_cast(flat2(moe_w_gate[0]), N_EXPERTS, ni),
                        _flat_cast(flat2(moe_w_down[0]), ni * nf, lambda i, f: i * nf + f)])
    yl, yr, (moe_u,) = branches(
        1, xf, wi1, [_tiled_cast(flat2(moe_w_up[0]), N_EXPERTS, n_steps, step)])
    router = jnp.pad(moe_router[0], ((0, 0), (0, LANES - N_EXPERTS))).astype(BF16)
    x1, hp, idx, gates = _out_proj(yl, yr, xf, wo1, row(norm2_g[1]), router)
    tm = min(FFN_TM, t)
    pos, src3, tile_expert, tile_valid = _routing_plan(idx[:, :TOP_K], tm)
    ys = _moe_ffn(hp, tile_expert, tile_valid, src3,
                  moe_g, moe_u, moe_d.reshape(moe_w_down.shape[1:]))
    tt = min(COMB_TT, t)
    pos3 = pos.reshape(t // tt, 1, tt * TOP_K)
    out = _combine(x1, gates, pos3, ys, row(final_norm_g))
    return out.reshape(b, s, d)
```
